```python
import jax, jax.numpy as jnp
from jax import lax
import numpy as np

D_MODEL = 1024
BATCH = 2
SEQ = 16384
DEPTH = 4

N_MIXERS = 2
D_MIX = D_MODEL
MEM_LEN = 256
MEM_HEADS = 4
MEM_HEAD_DIM = 64
MEM_WIDTH = MEM_HEADS * MEM_HEAD_DIM
TOK_WIDTH = D_MIX - MEM_WIDTH
LRU_WIDTH = TOK_WIDTH
LRU_BLOCKS = 8
LRU_BLOCK_DIM = LRU_WIDTH // LRU_BLOCKS
CONV_WIDTH = 4
CONV_PAD = (2, 1)
LRU_C = 8.0
MLA_HEADS = 12
QK_NOPE_DIM = 64
QK_ROPE_DIM = 32
QK_DIM = QK_NOPE_DIM + QK_ROPE_DIM
V_HEAD_DIM = TOK_WIDTH // MLA_HEADS
Q_LORA_RANK = 384
KV_LORA_RANK = 256
ROPE_THETA = 10000.0
Q_BLOCK = 128
D_FF = 2816
EPS = 1e-6
N_LRU_LAYERS = (DEPTH + 1) // 2
N_MLA_LAYERS = DEPTH // 2
LRU_IN_WIDTH = 2 * LRU_WIDTH + MEM_WIDTH
MLA_TOK_IN_WIDTH = Q_LORA_RANK + KV_LORA_RANK + QK_ROPE_DIM
MLA_IN_WIDTH = MLA_TOK_IN_WIDTH + MEM_WIDTH

kernel_name = 'hybrid_rglru_mla_macaron_encoder'


def rms_norm(x, g):
    xf = x.astype(jnp.float32)
    y = xf * lax.rsqrt(jnp.mean(xf * xf, axis=-1, keepdims=True) + EPS)
    return (y * g.astype(jnp.float32)).astype(x.dtype)


def swiglu_ffn(x, w_gate_up, w_down):
    g, u = jnp.split(x @ w_gate_up, [D_FF], axis=-1)
    return (jax.nn.silu(g) * u) @ w_down


def rope_tables(positions):
    half = QK_ROPE_DIM // 2
    inv_freq = ROPE_THETA ** (-jnp.arange(half, dtype=jnp.float32) * (2.0 / QK_ROPE_DIM))
    ang = positions.astype(jnp.float32)[..., None] * inv_freq
    return jnp.cos(ang)[:, :, None, :], jnp.sin(ang)[:, :, None, :]


def apply_rope(x, cos, sin):
    half = QK_ROPE_DIM // 2
    xf = x.astype(jnp.float32)
    x1, x2 = xf[..., :half], xf[..., half:]
    return jnp.concatenate([x1 * cos - x2 * sin, x2 * cos + x1 * sin], axis=-1).astype(x.dtype)


def memory_attention(q_in, mem_n, w_mem_kv, q_gain, k_gain):
    B, S, _ = q_in.shape
    q = rms_norm(q_in.reshape(B, S, MEM_HEADS, MEM_HEAD_DIM), q_gain)
    k, v = jnp.split(mem_n @ w_mem_kv, [MEM_WIDTH], axis=-1)
    k = rms_norm(k.reshape(B, -1, MEM_HEADS, MEM_HEAD_DIM), k_gain)
    v = v.reshape(B, -1, MEM_HEADS, MEM_HEAD_DIM)
    s = jnp.einsum('bshd,bmhd->bhsm', q.astype(jnp.float32), k.astype(jnp.float32)) * (MEM_HEAD_DIM ** -0.5)
    p = jax.nn.softmax(s, axis=-1).astype(v.dtype)
    return jnp.einsum('bhsm,bmhd->bshd', p, v).reshape(B, S, MEM_WIDTH)


def _linear_combine(left, right):
    a1, b1 = left
    a2, b2 = right
    return a1 * a2, a2 * b1 + b2


def rglru_direction(xc, gate_w, gate_b, lam, reverse):
    B, S, _ = xc.shape
    xb = xc.reshape(B, S, LRU_BLOCKS, LRU_BLOCK_DIM)
    gates = jnp.einsum('bsnk,gnkj->gbsnj', xb, gate_w).reshape(2, B, S, LRU_WIDTH)
    gates = gates.astype(jnp.float32) + gate_b.astype(jnp.float32)[:, None, None, :]
    r_gate = jax.nn.sigmoid(gates[0])
    i_gate = jax.nn.sigmoid(gates[1])
    log_a = -LRU_C * r_gate * jax.nn.softplus(-lam.astype(jnp.float32))
    a = jnp.exp(log_a)
    b = jnp.sqrt(-jnp.expm1(2.0 * log_a)) * (i_gate * xc.astype(jnp.float32))
    _, h = lax.associative_scan(_linear_combine, (a, b), reverse=reverse, axis=1)
    return h


def rglru_mixer(u, conv_w, conv_b, gate_w, gate_b, lam):
    gate_branch, xr = jnp.split(u, [LRU_WIDTH], axis=-1)
    xc = lax.conv_general_dilated(
        xr, conv_w[:, None, :].astype(xr.dtype), window_strides=(1,), padding=[CONV_PAD],
        dimension_numbers=('NWC', 'WIO', 'NWC'), feature_group_count=LRU_WIDTH) + conv_b
    h = (rglru_direction(xc, gate_w[0], gate_b[0], lam[0], False)
         + rglru_direction(xc, gate_w[1], gate_b[1], lam[1], True))
    return h.astype(u.dtype) * jax.nn.gelu(gate_branch)


def dense_attention(q, k, v):
    B, S, H, D = q.shape
    nb = S // Q_BLOCK
    qb = q.reshape(B, nb, Q_BLOCK, H, D).transpose(1, 0, 2, 3, 4)
    kf = k.astype(jnp.float32)
    scale = QK_DIM ** -0.5

    def one_block(q_blk):
        s = jnp.einsum('bqhd,bkhd->bhqk', q_blk.astype(jnp.float32), kf) * scale
        p = jax.nn.softmax(s, axis=-1).astype(v.dtype)
        return jnp.einsum('bhqk,bkhv->bqhv', p, v)

    o = lax.map(one_block, qb)
    return o.transpose(1, 0, 2, 3, 4).reshape(B, S, H * V_HEAD_DIM)


def mla_mixer(u, q_a_norm, w_uq, kv_a_norm, w_ukv, q_norm, k_norm, cos, sin):
    B, S, _ = u.shape
    c_q, c_kv, k_rope = jnp.split(u, [Q_LORA_RANK, Q_LORA_RANK + KV_LORA_RANK], axis=-1)
    q = (rms_norm(c_q, q_a_norm) @ w_uq).reshape(B, S, MLA_HEADS, QK_DIM)
    kv = (rms_norm(c_kv, kv_a_norm) @ w_ukv).reshape(B, S, MLA_HEADS, QK_NOPE_DIM + V_HEAD_DIM)
    k_nope, v = jnp.split(kv, [QK_NOPE_DIM], axis=-1)
    k_rope = jnp.broadcast_to(k_rope[:, :, None, :], (B, S, MLA_HEADS, QK_ROPE_DIM))
    k = jnp.concatenate([k_nope, k_rope], axis=-1)
    q = rms_norm(q, q_norm)
    k = rms_norm(k, k_norm)
    q = jnp.concatenate([q[..., :QK_NOPE_DIM], apply_rope(q[..., QK_NOPE_DIM:], cos, sin)], axis=-1)
    k = jnp.concatenate([k[..., :QK_NOPE_DIM], apply_rope(k[..., QK_NOPE_DIM:], cos, sin)], axis=-1)
    return dense_attention(q, k, v)


def setup_inputs(seed: int = 0) -> dict:
    key = jax.random.key(seed)
    ks = iter(jax.random.split(key, 40))

    def w(shape, fan_in):
        return jax.random.normal(next(ks), shape, jnp.float32) * (fan_in ** -0.5)

    def gain(shape):
        return 1.0 + 0.02 * jax.random.normal(next(ks), shape, jnp.float32)

    def bias(shape):
        return 0.01 * jax.random.normal(next(ks), shape, jnp.float32)

    x = jax.random.normal(next(ks), (BATCH, SEQ, D_MODEL), jnp.float32)
    mem = jax.random.normal(next(ks), (BATCH, MEM_LEN, D_MODEL), jnp.float32)
    positions = jnp.broadcast_to(jnp.arange(SEQ, dtype=jnp.int32), (BATCH, SEQ))
    a_c = jax.random.uniform(next(ks), (N_LRU_LAYERS, 2, LRU_WIDTH), jnp.float32, 0.9, 0.999)
    a0 = a_c ** (1.0 / LRU_C)
    lru_lambda = jnp.log(a0) - jnp.log1p(-a0)
    return {
        'x': x,
        'mem': mem,
        'positions': positions,
        'ffn1_norm': gain((DEPTH, D_MODEL)),
        'ffn1_w_gate_up': w((DEPTH, D_MODEL, 2 * D_FF), D_MODEL),
        'ffn1_w_down': w((DEPTH, D_FF, D_MODEL), D_FF),
        'mix_norm': gain((DEPTH, D_MODEL)),
        'mem_norm': gain((DEPTH, D_MODEL)),
        'w_mem_kv': w((DEPTH, D_MODEL, 2 * MEM_WIDTH), D_MODEL),
        'mem_q_norm': gain((DEPTH, MEM_HEAD_DIM)),
        'mem_k_norm': gain((DEPTH, MEM_HEAD_DIM)),
        'w_out': w((DEPTH, D_MIX, D_MODEL), D_MIX),
        'ffn2_norm': gain((DEPTH, D_MODEL)),
        'ffn2_w_gate_up': w((DEPTH, D_MODEL, 2 * D_FF), D_MODEL),
        'ffn2_w_down': w((DEPTH, D_FF, D_MODEL), D_FF),
        'lru_w_in': w((N_LRU_LAYERS, D_MODEL, LRU_IN_WIDTH), D_MODEL),
        'lru_conv_w': w((N_LRU_LAYERS, CONV_WIDTH, LRU_WIDTH), CONV_WIDTH),
        'lru_conv_b': bias((N_LRU_LAYERS, LRU_WIDTH)),
        'lru_gate_w': w((N_LRU_LAYERS, 2, 2, LRU_BLOCKS, LRU_BLOCK_DIM, LRU_BLOCK_DIM), LRU_BLOCK_DIM),
        'lru_gate_b': bias((N_LRU_LAYERS, 2, 2, LRU_WIDTH)),
        'lru_lambda': lru_lambda,
        'mla_w_in': w((N_MLA_LAYERS, D_MODEL, MLA_IN_WIDTH), D_MODEL),
        'mla_q_a_norm': gain((N_MLA_LAYERS, Q_LORA_RANK)),
        'mla_w_uq': w((N_MLA_LAYERS, Q_LORA_RANK, MLA_HEADS * QK_DIM), Q_LORA_RANK),
        'mla_kv_a_norm': gain((N_MLA_LAYERS, KV_LORA_RANK)),
        'mla_w_ukv': w((N_MLA_LAYERS, KV_LORA_RANK, MLA_HEADS * (QK_NOPE_DIM + V_HEAD_DIM)), KV_LORA_RANK),
        'mla_q_norm': gain((N_MLA_LAYERS, QK_DIM)),
        'mla_k_norm': gain((N_MLA_LAYERS, QK_DIM)),
    }


def reference(x, mem, positions, ffn1_norm, ffn1_w_gate_up, ffn1_w_down, mix_norm, mem_norm,
              w_mem_kv, mem_q_norm, mem_k_norm, w_out, ffn2_norm, ffn2_w_gate_up, ffn2_w_down,
              lru_w_in, lru_conv_w, lru_conv_b, lru_gate_w, lru_gate_b, lru_lambda,
              mla_w_in, mla_q_a_norm, mla_w_uq, mla_kv_a_norm, mla_w_ukv, mla_q_norm, mla_k_norm):
    cos, sin = rope_tables(positions)
    for layer in range(DEPTH):
        x = x + 0.5 * swiglu_ffn(rms_norm(x, ffn1_norm[layer]), ffn1_w_gate_up[layer], ffn1_w_down[layer])
        h = rms_norm(x, mix_norm[layer])
        mem_n = rms_norm(mem, mem_norm[layer])
        j = layer // N_MIXERS
        if layer % N_MIXERS == 0:
            u = h @ lru_w_in[j]
            u_tok, u_mem = jnp.split(u, [2 * LRU_WIDTH], axis=-1)
            tok = rglru_mixer(u_tok, lru_conv_w[j], lru_conv_b[j], lru_gate_w[j],
                              lru_gate_b[j], lru_lambda[j])
        else:
            u = h @ mla_w_in[j]
            u_tok, u_mem = jnp.split(u, [MLA_TOK_IN_WIDTH], axis=-1)
            tok = mla_mixer(u_tok, mla_q_a_norm[j], mla_w_uq[j], mla_kv_a_norm[j], mla_w_ukv[j],
                            mla_q_norm[j], mla_k_norm[j], cos, sin)
        mem_out = memory_attention(u_mem, mem_n, w_mem_kv[layer], mem_q_norm[layer], mem_k_norm[layer])
        x = x + jnp.concatenate([tok, mem_out], axis=-1) @ w_out[layer]
        x = x + 0.5 * swiglu_ffn(rms_norm(x, ffn2_norm[layer]), ffn2_w_gate_up[layer], ffn2_w_down[layer])
    return x
```

```python
import functools
import math

import jax
import jax.numpy as jnp
from jax import lax
from jax.experimental import pallas as pl
from jax.experimental.pallas import tpu as pltpu

F32 = jnp.float32
BF16 = jnp.bfloat16

D_MODEL = 1024
MEM_LEN = 256
MEM_HEADS = 4
MEM_HEAD_DIM = 64
MEM_WIDTH = MEM_HEADS * MEM_HEAD_DIM
TOK_WIDTH = D_MODEL - MEM_WIDTH
LRU_WIDTH = TOK_WIDTH
LRU_BLOCKS = 8
LRU_BLOCK_DIM = LRU_WIDTH // LRU_BLOCKS
CONV_WIDTH = 4
CONV_LEFT = 2
LRU_C = 8.0
MLA_HEADS = 12
QK_NOPE_DIM = 64
QK_ROPE_DIM = 32
QK_DIM = QK_NOPE_DIM + QK_ROPE_DIM
V_HEAD_DIM = TOK_WIDTH // MLA_HEADS
Q_LORA_RANK = 384
KV_LORA_RANK = 256
ROPE_THETA = 10000.0
D_FF = 2816
EPS = 1e-6

LANES = 128
SUBLANES = 8
HEAD_PAD = LANES
ROPE_HALF = QK_ROPE_DIM // 2
VMEM_LIMIT = 56 * 1024 * 1024
LOG2E = math.log2(math.e)


def _params(sem, vmem=VMEM_LIMIT):
    return pltpu.CompilerParams(dimension_semantics=sem, vmem_limit_bytes=vmem)


def _rms(x, g):
    ms = jnp.mean(x * x, axis=-1, keepdims=True)
    return x * lax.rsqrt(ms + EPS) * g


def _full(shape):
    zeros = (0,) * len(shape)
    return pl.BlockSpec(shape, lambda *_: zeros)


def _token_tile(t):
    for tm in (512, 256, 128, 64, 32, 16, 8):
        if t % tm == 0:
            return tm
    raise ValueError(f"token count {t} must be a multiple of 8")


def _rms_proj_kernel(x_ref, g_ref, w_ref, *o_refs, splits):
    h = _rms(x_ref[...], g_ref[...]).astype(BF16)
    y = jnp.dot(h, w_ref[...], preferred_element_type=F32)
    off = 0
    for o_ref, n in zip(o_refs, splits):
        o_ref[...] = y[:, off:off + n].astype(o_ref.dtype)
        off += n


def _rms_proj(x, g, w, splits, name):
    t, d = x.shape
    n = w.shape[1]
    assert sum(splits) == n and all(s % LANES == 0 for s in splits)
    tm = _token_tile(t)
    return pl.pallas_call(
        functools.partial(_rms_proj_kernel, splits=splits),
        grid=(t // tm,),
        in_specs=[pl.BlockSpec((tm, d), lambda i: (i, 0)), _full((1, d)), _full((d, n))],
        out_specs=[pl.BlockSpec((tm, s), lambda i: (i, 0)) for s in splits],
        out_shape=[jax.ShapeDtypeStruct((t, s), F32) for s in splits],
        compiler_params=_params(("parallel",)),
        name=name,
    )(x, g.reshape(1, d), w)


def _ffn_kernel(x_ref, g_ref, wgu_ref, wd_ref, o_ref):
    x = x_ref[...]
    h = _rms(x, g_ref[...]).astype(BF16)
    gu = jnp.dot(h, wgu_ref[...], preferred_element_type=F32)
    gate, up = gu[:, :D_FF], gu[:, D_FF:]
    act = (gate * jax.nn.sigmoid(gate) * up).astype(BF16)
    y = jnp.dot(act, wd_ref[...], preferred_element_type=F32)
    o_ref[...] = x + 0.5 * y


def _ffn(x, g, wgu, wd, name):
    t, d = x.shape
    tm = min(_token_tile(t), 256)
    once = pl.Buffered(1)
    return pl.pallas_call(
        _ffn_kernel,
        grid=(t // tm,),
        in_specs=[
            pl.BlockSpec((tm, d), lambda i: (i, 0)),
            _full((1, d)),
            pl.BlockSpec(wgu.shape, lambda i: (0, 0), pipeline_mode=once),
            pl.BlockSpec(wd.shape, lambda i: (0, 0), pipeline_mode=once),
        ],
        out_specs=pl.BlockSpec((tm, d), lambda i: (i, 0)),
        out_shape=jax.ShapeDtypeStruct((t, d), F32),
        compiler_params=_params(("parallel",)),
        name=name,
    )(x, g.reshape(1, d), wgu, wd)


def _head_rms_scale(x, head_dim):
    lane = lax.broadcasted_iota(jnp.int32, x.shape, 1)
    x2 = x * x
    scale = jnp.zeros_like(x)
    for h in range(x.shape[1] // head_dim):
        in_head = (lane >= h * head_dim) & (lane < (h + 1) * head_dim)
        ss = jnp.sum(jnp.where(in_head, x2, 0.0), axis=-1, keepdims=True)
        scale = jnp.where(in_head, lax.rsqrt(ss * (1.0 / head_dim) + EPS), scale)
    return scale


def _mem_kv_kernel(mem_ref, g_ref, w_ref, kg_ref, kt_ref, v_ref):
    mn = _rms(mem_ref[0], g_ref[...]).astype(BF16)
    kv = jnp.dot(mn, w_ref[...], preferred_element_type=F32)
    k, v = kv[:, :MEM_WIDTH], kv[:, MEM_WIDTH:]
    kn = k * _head_rms_scale(k, MEM_HEAD_DIM) * kg_ref[...]
    kt_ref[0] = kn.T.astype(BF16)
    v_ref[0] = v.astype(BF16)


def _mem_kv(mem, g, w, k_gain, name):
    b, m, d = mem.shape
    return pl.pallas_call(
        _mem_kv_kernel,
        grid=(b,),
        in_specs=[pl.BlockSpec((1, m, d), lambda i: (i, 0, 0)), _full((1, d)),
                  _full(w.shape), _full((1, MEM_WIDTH))],
        out_specs=[pl.BlockSpec((1, MEM_WIDTH, m), lambda i: (i, 0, 0)),
                   pl.BlockSpec((1, m, MEM_WIDTH), lambda i: (i, 0, 0))],
        out_shape=[jax.ShapeDtypeStruct((b, MEM_WIDTH, m), BF16),
                   jax.ShapeDtypeStruct((b, m, MEM_WIDTH), BF16)],
        compiler_params=_params(("parallel",)),
        name=name,
    )(mem, g.reshape(1, d), w, jnp.tile(k_gain, MEM_HEADS).reshape(1, MEM_WIDTH))


def _gelu_tanh(x):
    return 0.5 * x * (1.0 + jnp.tanh(math.sqrt(2.0 / math.pi) * (x + 0.044715 * (x * x * x))))


def _mem_attention(q, qg, kt, v):
    lane = lax.broadcasted_iota(jnp.int32, q.shape, 1)
    qn = q * _head_rms_scale(q, MEM_HEAD_DIM) * (qg * (MEM_HEAD_DIM ** -0.5))
    out = jnp.zeros_like(q)
    for h in range(MEM_HEADS):
        in_head = (lane >= h * MEM_HEAD_DIM) & (lane < (h + 1) * MEM_HEAD_DIM)
        qh = jnp.where(in_head, qn, 0.0).astype(BF16)
        s = jnp.dot(qh, kt, preferred_element_type=F32)
        p = jnp.exp(s - jnp.max(s, axis=-1, keepdims=True))
        denom = jnp.sum(p, axis=-1, keepdims=True)
        oh = jnp.dot(p.astype(BF16), v, preferred_element_type=F32)
        out = jnp.where(in_head, oh / denom, out)
    return out


def _mix_out_lru_kernel(x_ref, hf_ref, hr_ref, gate_ref, um_ref, qg_ref, kt_ref, v_ref,
                        wt_ref, wm_ref, o_ref):
    tok = (hf_ref[0] + hr_ref[0]) * _gelu_tanh(gate_ref[...])
    mem_out = _mem_attention(um_ref[...], qg_ref[...], kt_ref[0], v_ref[0])
    y = jnp.dot(tok.astype(BF16), wt_ref[...], preferred_element_type=F32)
    y = y + jnp.dot(mem_out.astype(BF16), wm_ref[...], preferred_element_type=F32)
    o_ref[...] = x_ref[...] + y


def _mix_out_mla_kernel(x_ref, tok_ref, um_ref, qg_ref, kt_ref, v_ref, wt_ref, wm_ref, o_ref):
    mem_out = _mem_attention(um_ref[...], qg_ref[...], kt_ref[0], v_ref[0])
    y = jnp.dot(tok_ref[...], wt_ref[...], preferred_element_type=F32)
    y = y + jnp.dot(mem_out.astype(BF16), wm_ref[...], preferred_element_type=F32)
    o_ref[...] = x_ref[...] + y


def _mix_out(x, toks, u_mem, q_gain, kt, v, w_out, seq, name):
    t, d = x.shape
    tm = _token_tile(seq)
    per_batch = seq // tm
    row = lambda w: pl.BlockSpec((tm, w), lambda i: (i, 0))
    per_b = lambda shape: pl.BlockSpec((1,) + shape, lambda i: (i // per_batch, 0, 0))
    common_specs = [row(MEM_WIDTH), _full((1, MEM_WIDTH)), per_b(kt.shape[1:]), per_b(v.shape[1:]),
                    _full((TOK_WIDTH, d)), _full((MEM_WIDTH, d))]
    common_args = (u_mem, jnp.tile(q_gain, MEM_HEADS).reshape(1, MEM_WIDTH), kt, v,
                   w_out[:TOK_WIDTH], w_out[TOK_WIDTH:])
    if len(toks) == 2:
        h_dirs, gate = toks
        kern = _mix_out_lru_kernel
        tok_specs = [pl.BlockSpec((1, tm, TOK_WIDTH), lambda i: (0, i, 0)),
                     pl.BlockSpec((1, tm, TOK_WIDTH), lambda i: (1, i, 0)), row(TOK_WIDTH)]
        tok_args = (h_dirs, h_dirs, gate)
    else:
        kern = _mix_out_mla_kernel
        tok_specs = [row(TOK_WIDTH)]
        tok_args = toks
    return pl.pallas_call(
        kern,
        grid=(t // tm,),
        in_specs=[row(d)] + tok_specs + common_specs,
        out_specs=row(d),
        out_shape=jax.ShapeDtypeStruct((t, d), F32),
        compiler_params=_params(("parallel",)),
        name=name,
    )(x, *tok_args, *common_args)


def _softplus(x):
    return jnp.maximum(x, 0.0) + jnp.log1p(jnp.exp(-jnp.abs(x)))


def _group_scan(a, b, reverse):
    row = lax.broadcasted_iota(jnp.int32, a.shape, 0)
    for dist in (1, 2, 4):
        if reverse:
            valid = row < SUBLANES - dist
            shift = SUBLANES - dist
        else:
            valid = row >= dist
            shift = dist
        a_sh = jnp.where(valid, pltpu.roll(a, shift, 0), 1.0)
        b_sh = jnp.where(valid, pltpu.roll(b, shift, 0), 0.0)
        b = a * b_sh + b
        a = a * a_sh
    return a, b


def _lru_kernel(cur_ref, prev_ref, next_ref, cw_ref, cb_ref, gw_ref, gb_ref, lam_ref, o_ref,
                ext_sc, a_sc, b_sc, h_sc, *, tc):
    direction = pl.program_id(0)
    step = pl.program_id(2)
    n_chunks = pl.num_programs(2)
    chunk = jnp.where(direction == 0, step, n_chunks - 1 - step)

    @pl.when(step == 0)
    def _():
        h_sc[...] = jnp.zeros_like(h_sc)

    ext_sc[0:SUBLANES, :] = jnp.where(chunk > 0, prev_ref[0], 0.0)
    ext_sc[SUBLANES:SUBLANES + tc, :] = cur_ref[0]
    ext_sc[SUBLANES + tc:, :] = jnp.where(chunk < n_chunks - 1, next_ref[0], 0.0)
    xc = jnp.broadcast_to(cb_ref[...], (tc, LRU_WIDTH))
    for tap in range(CONV_WIDTH):
        xc = xc + cw_ref[tap:tap + 1, :] * ext_sc[pl.ds(SUBLANES - CONV_LEFT + tap, tc), :]

    z = jnp.dot(xc.astype(BF16), gw_ref[0], preferred_element_type=F32)
    r_gate = jax.nn.sigmoid(z[:, :LRU_WIDTH] + gb_ref[0, 0:1, :])
    i_gate = jax.nn.sigmoid(z[:, LRU_WIDTH:] + gb_ref[0, 1:2, :])
    log_a = (-LRU_C) * r_gate * _softplus(-lam_ref[0])
    a = jnp.exp(log_a)
    a_sc[...] = a
    b_sc[...] = jnp.sqrt(-jnp.tanh(log_a) * (a * a + 1.0)) * (i_gate * xc)

    n_groups = tc // SUBLANES

    def run(reverse):
        def body(g, h):
            grp = n_groups - 1 - g if reverse else g
            off = pl.multiple_of(grp * SUBLANES, SUBLANES)
            a, b = _group_scan(a_sc[pl.ds(off, SUBLANES), :], b_sc[pl.ds(off, SUBLANES), :], reverse)
            hs = a * h + b
            o_ref[0, 0, pl.ds(off, SUBLANES), :] = hs
            last = hs[0:1, :] if reverse else hs[SUBLANES - 1:SUBLANES, :]
            return jnp.broadcast_to(last, hs.shape)
        h_sc[...] = lax.fori_loop(0, n_groups, body, h_sc[...])

    @pl.when(direction == 0)
    def _():
        run(False)

    @pl.when(direction == 1)
    def _():
        run(True)


def _lru_scan(xr, conv_w, conv_b, gate_w, gate_b, lam, name):
    b, s, w = xr.shape
    tc = _token_tile(s)
    n_chunks = s // tc
    per_chunk = tc // SUBLANES
    n_rows8 = s // SUBLANES

    def chunk_of(d, i):
        return jnp.where(d == 0, i, n_chunks - 1 - i)

    return pl.pallas_call(
        functools.partial(_lru_kernel, tc=tc),
        grid=(2, b, n_chunks),
        in_specs=[
            pl.BlockSpec((1, tc, w), lambda d, bi, i: (bi, chunk_of(d, i), 0)),
            pl.BlockSpec((1, SUBLANES, w),
                         lambda d, bi, i: (bi, jnp.maximum(chunk_of(d, i) * per_chunk - 1, 0), 0)),
            pl.BlockSpec((1, SUBLANES, w),
                         lambda d, bi, i: (bi, jnp.minimum((chunk_of(d, i) + 1) * per_chunk, n_rows8 - 1), 0)),
            _full((CONV_WIDTH, w)),
            _full((1, w)),
            pl.BlockSpec((1, w, 2 * w), lambda d, bi, i: (d, 0, 0)),
            pl.BlockSpec((1, 2, w), lambda d, bi, i: (d, 0, 0)),
            pl.BlockSpec((1, 1, w), lambda d, bi, i: (d, 0, 0)),
        ],
        out_specs=pl.BlockSpec((1, 1, tc, w), lambda d, bi, i: (d, bi, chunk_of(d, i), 0)),
        out_shape=jax.ShapeDtypeStruct((2, b, s, w), F32),
        scratch_shapes=[pltpu.VMEM((tc + 2 * SUBLANES, w), F32), pltpu.VMEM((tc, w), F32),
                        pltpu.VMEM((tc, w), F32), pltpu.VMEM((SUBLANES, w), F32)],
        compiler_params=_params(("arbitrary", "arbitrary", "arbitrary")),
        name=name,
    )(xr, xr, xr, conv_w, conv_b.reshape(1, w), gate_w, gate_b, lam.reshape(2, 1, w))


def _dense_block_diag(blocks):
    nb, bw, _ = blocks.shape
    eye = jnp.eye(nb, dtype=blocks.dtype)
    return (eye[:, None, :, None] * blocks[:, :, None, :]).reshape(nb * bw, nb * bw)


def _rope_table_kernel(pos_ref, freq_ref, c_ref, sa_ref, sb_ref):
    ang = pos_ref[...].astype(F32) * freq_ref[...]
    lane = lax.broadcasted_iota(jnp.int32, ang.shape, 1)
    cos, sin = jnp.cos(ang), jnp.sin(ang)
    rope_lo = QK_NOPE_DIM
    rope_mid = QK_NOPE_DIM + ROPE_HALF
    c_ref[...] = jnp.where(lane < rope_lo, 1.0, jnp.where(lane < QK_DIM, cos, 0.0))
    sa_ref[...] = jnp.where(lane < rope_lo, 0.0, jnp.where(lane < rope_mid, -sin, 0.0))
    sb_ref[...] = jnp.where(lane < rope_mid, 0.0, jnp.where(lane < QK_DIM, sin, 0.0))


def _rope_tables(positions):
    t = positions.size
    tm = _token_tile(t)
    inv_freq = ROPE_THETA ** (-jnp.arange(ROPE_HALF, dtype=F32) * (2.0 / QK_ROPE_DIM))
    freq = jnp.zeros((HEAD_PAD,), F32).at[QK_NOPE_DIM:QK_DIM].set(jnp.tile(inv_freq, 2))
    spec = pl.BlockSpec((tm, HEAD_PAD), lambda i: (i, 0))
    return pl.pallas_call(
        _rope_table_kernel,
        grid=(t // tm,),
        in_specs=[pl.BlockSpec((tm, 1), lambda i: (i, 0)), _full((1, HEAD_PAD))],
        out_specs=[spec, spec, spec],
        out_shape=[jax.ShapeDtypeStruct((t, HEAD_PAD), F32)] * 3,
        compiler_params=_params(("parallel",)),
        name="rope_tables",
    )(positions.reshape(t, 1), freq.reshape(1, HEAD_PAD))


def _rope(x, c, sa, sb):
    return x * c + pltpu.roll(x, HEAD_PAD - ROPE_HALF, 1) * sa + pltpu.roll(x, ROPE_HALF, 1) * sb


def _mla_prep_kernel(cq_ref, ckv_ref, kr_ref, c_ref, sa_ref, sb_ref, qa_ref, kva_ref, qg_ref, kg_ref,
                     wuq_ref, wukv_ref, qt_ref, k_ref, vt_ref):
    yq = jnp.dot(_rms(cq_ref[...], qa_ref[...]).astype(BF16), wuq_ref[...], preferred_element_type=F32)
    ykv = jnp.dot(_rms(ckv_ref[...], kva_ref[...]).astype(BF16), wukv_ref[...], preferred_element_type=F32)
    k_rope = kr_ref[...]
    c, sa, sb = c_ref[...], sa_ref[...], sb_ref[...]
    q_gain = qg_ref[...] * ((QK_DIM ** -0.5) * LOG2E)
    k_gain = kg_ref[...]
    lane = lax.broadcasted_iota(jnp.int32, k_rope.shape, 1)
    inv_dim = 1.0 / QK_DIM
    for h in range(MLA_HEADS):
        cols = slice(h * HEAD_PAD, (h + 1) * HEAD_PAD)
        qh = yq[:, cols]
        qn = qh * lax.rsqrt(jnp.sum(qh * qh, axis=-1, keepdims=True) * inv_dim + EPS) * q_gain
        qt_ref[0, h] = _rope(qn, c, sa, sb).T.astype(BF16)
        kvh = ykv[:, cols]
        kf = jnp.where(lane < QK_NOPE_DIM, kvh, k_rope)
        kn = kf * lax.rsqrt(jnp.sum(kf * kf, axis=-1, keepdims=True) * inv_dim + EPS) * k_gain
        k_ref[0, :, cols] = _rope(kn, c, sa, sb).astype(BF16)
        vt_ref[0, h, 0] = kvh.T[QK_NOPE_DIM:, :].astype(BF16)


def _mla_prep(cq, ckv, kr, tables, qa, kva, qg, kg, wuq, wukv, batch, seq, tk, name):
    n_kv = seq // tk
    row = lambda w: pl.BlockSpec((tk, w), lambda b, i: (b * n_kv + i, 0))
    width = MLA_HEADS * HEAD_PAD
    pad = lambda g: jnp.pad(g, (0, HEAD_PAD - QK_DIM)).reshape(1, HEAD_PAD)
    return pl.pallas_call(
        _mla_prep_kernel,
        grid=(batch, n_kv),
        in_specs=[row(Q_LORA_RANK), row(KV_LORA_RANK), row(HEAD_PAD), row(HEAD_PAD), row(HEAD_PAD),
                  row(HEAD_PAD), _full((1, Q_LORA_RANK)), _full((1, KV_LORA_RANK)),
                  _full((1, HEAD_PAD)), _full((1, HEAD_PAD)), _full(wuq.shape), _full(wukv.shape)],
        out_specs=[pl.BlockSpec((1, MLA_HEADS, HEAD_PAD, tk), lambda b, i: (b, 0, 0, i)),
                   pl.BlockSpec((1, tk, width), lambda b, i: (b, i, 0)),
                   pl.BlockSpec((1, MLA_HEADS, 1, V_HEAD_DIM, tk), lambda b, i: (b, 0, i, 0, 0))],
        out_shape=[jax.ShapeDtypeStruct((batch, MLA_HEADS, HEAD_PAD, seq), BF16),
                   jax.ShapeDtypeStruct((batch, seq, width), BF16),
                   jax.ShapeDtypeStruct((batch, MLA_HEADS, n_kv, V_HEAD_DIM, tk), BF16)],
        compiler_params=_params(("parallel", "parallel")),
        name=name,
    )(cq, ckv, kr, *tables, qa.reshape(1, -1), kva.reshape(1, -1), pad(qg), pad(kg), wuq, wukv)


def _attn_kernel(qt_ref, k_ref, vt_ref, o_ref, *, tk, n_kv):
    tq = qt_ref.shape[-1]
    q_t = [qt_ref[0, hh] for hh in range(2)]

    def body(j, carry):
        off = pl.multiple_of(j * tk, tk)
        new = []
        for hh in range(2):
            m_prev, l_prev, acc = carry[hh]
            k = k_ref[0, pl.ds(off, tk), hh * HEAD_PAD:(hh + 1) * HEAD_PAD]
            s_t = jnp.dot(k, q_t[hh], preferred_element_type=F32)
            m_new = jnp.maximum(m_prev, jnp.max(s_t, axis=0, keepdims=True))
            alpha = jnp.exp2(m_prev - m_new)
            p = jnp.exp2(s_t - m_new)
            l_new = alpha * l_prev + jnp.sum(p, axis=0, keepdims=True)
            pv = jnp.dot(vt_ref[0, hh, j], p.astype(BF16), preferred_element_type=F32)
            new.append((m_new, l_new, alpha * acc + pv))
        return tuple(new)

    init = tuple((jnp.full((1, tq), -jnp.inf, F32), jnp.zeros((1, tq), F32),
                  jnp.zeros((V_HEAD_DIM, tq), F32)) for _ in range(2))
    final = lax.fori_loop(0, n_kv, body, init)
    out_t = jnp.concatenate([acc / l for (_, l, acc) in final], axis=0)
    o_ref[0] = out_t.T.astype(o_ref.dtype)


def _attention(q_t, k, v_t, tq, name):
    batch, heads, _, seq = q_t.shape
    n_kv, tk = v_t.shape[2], v_t.shape[4]
    pairs = heads // 2
    return pl.pallas_call(
        functools.partial(_attn_kernel, tk=tk, n_kv=n_kv),
        grid=(batch, pairs, seq // tq),
        in_specs=[pl.BlockSpec((1, 2, HEAD_PAD, tq), lambda b, hp, qi: (b, hp, 0, qi)),
                  pl.BlockSpec((1, seq, 2 * HEAD_PAD), lambda b, hp, qi: (b, 0, hp)),
                  pl.BlockSpec((1, 2, n_kv, V_HEAD_DIM, tk), lambda b, hp, qi: (b, hp, 0, 0, 0))],
        out_specs=pl.BlockSpec((1, tq, 2 * V_HEAD_DIM), lambda b, hp, qi: (b, qi, hp)),
        out_shape=jax.ShapeDtypeStruct((batch, seq, heads * V_HEAD_DIM), BF16),
        compiler_params=_params(("parallel", "parallel", "arbitrary")),
        name=name,
    )(q_t, k, v_t)


def _mla_in_weight(w_in):
    c_q = w_in[:, :Q_LORA_RANK]
    c_kv = w_in[:, Q_LORA_RANK:Q_LORA_RANK + KV_LORA_RANK]
    k_rope = w_in[:, Q_LORA_RANK + KV_LORA_RANK:Q_LORA_RANK + KV_LORA_RANK + QK_ROPE_DIM]
    u_mem = w_in[:, Q_LORA_RANK + KV_LORA_RANK + QK_ROPE_DIM:]
    k_rope = jnp.pad(k_rope, ((0, 0), (QK_NOPE_DIM, HEAD_PAD - QK_DIM)))
    return jnp.concatenate([c_q, c_kv, u_mem, k_rope], axis=1).astype(BF16)


def _pad_heads(w, head_dim):
    r = w.shape[0]
    w = w.reshape(r, MLA_HEADS, head_dim)
    return jnp.pad(w, ((0, 0), (0, 0), (0, HEAD_PAD - head_dim))).reshape(r, MLA_HEADS * HEAD_PAD).astype(BF16)


def kernel(x, mem, positions, ffn1_norm, ffn1_w_gate_up, ffn1_w_down, mix_norm, mem_norm, w_mem_kv,
           mem_q_norm, mem_k_norm, w_out, ffn2_norm, ffn2_w_gate_up, ffn2_w_down, lru_w_in, lru_conv_w,
           lru_conv_b, lru_gate_w, lru_gate_b, lru_lambda, mla_w_in, mla_q_a_norm, mla_w_uq,
           mla_kv_a_norm, mla_w_ukv, mla_q_norm, mla_k_norm):
    batch, seq, d = x.shape
    depth = ffn1_norm.shape[0]
    t = batch * seq
    xf = x.reshape(t, d)
    tables = _rope_tables(positions)
    tk = _token_tile(seq)

    for layer in range(depth):
        j = layer // 2
        xf = _ffn(xf, ffn1_norm[layer], ffn1_w_gate_up[layer].astype(BF16),
                  ffn1_w_down[layer].astype(BF16), f"ffn1_{layer}")
        kt, v = _mem_kv(mem, mem_norm[layer], w_mem_kv[layer].astype(BF16), mem_k_norm[layer],
                        f"mem_kv_{layer}")
        if layer % 2 == 0:
            gate, xr, u_mem = _rms_proj(xf, mix_norm[layer], lru_w_in[j].astype(BF16),
                                        (LRU_WIDTH, LRU_WIDTH, MEM_WIDTH), f"lru_in_{layer}")
            gw = jnp.stack([
                jnp.concatenate([_dense_block_diag(lru_gate_w[j, dr, 0]),
                                 _dense_block_diag(lru_gate_w[j, dr, 1])], axis=1)
                for dr in range(2)]).astype(BF16)
            h_dirs = _lru_scan(xr.reshape(batch, seq, LRU_WIDTH), lru_conv_w[j], lru_conv_b[j], gw,
                               lru_gate_b[j], lru_lambda[j], f"lru_scan_{layer}")
            toks = (h_dirs.reshape(2, t, LRU_WIDTH), gate)
        else:
            cq, ckv, u_mem, kr = _rms_proj(xf, mix_norm[layer], _mla_in_weight(mla_w_in[j]),
                                           (Q_LORA_RANK, KV_LORA_RANK, MEM_WIDTH, HEAD_PAD),
                                           f"mla_in_{layer}")
            q_t, k, v_t = _mla_prep(cq, ckv, kr, tables, mla_q_a_norm[j], mla_kv_a_norm[j],
                                    mla_q_norm[j], mla_k_norm[j], _pad_heads(mla_w_uq[j], QK_DIM),
                                    mla_w_ukv[j].astype(BF16), batch, seq, tk, f"mla_prep_{layer}")
            attn = _attention(q_t, k, v_t, tk, f"mla_attn_{layer}")
            toks = (attn.reshape(t, TOK_WIDTH),)
        xf = _mix_out(xf, toks, u_mem, mem_q_norm[layer], kt, v, w_out[layer].astype(BF16), seq,
                      f"mix_out_{layer}")
        xf = _ffn(xf, ffn2_norm[layer], ffn2_w_gate_up[layer].astype(BF16),
                  ffn2_w_down[layer].astype(BF16), f"ffn2_{layer}")
    return xf.reshape(batch, seq, d)
```

```python
import functools
import math

import jax
import jax.numpy as jnp
from jax import lax
from jax.experimental import pallas as pl
from jax.experimental.pallas import tpu as pltpu

F32 = jnp.float32
BF16 = jnp.bfloat16

D_MODEL = 1024
MEM_LEN = 256
MEM_HEADS = 4
MEM_HEAD_DIM = 64
MEM_WIDTH = MEM_HEADS * MEM_HEAD_DIM
TOK_WIDTH = D_MODEL - MEM_WIDTH
LRU_WIDTH = TOK_WIDTH
LRU_BLOCKS = 8
LRU_BLOCK_DIM = LRU_WIDTH // LRU_BLOCKS
CONV_WIDTH = 4
CONV_LEFT = 2
LRU_C = 8.0
MLA_HEADS = 12
QK_NOPE_DIM = 64
QK_ROPE_DIM = 32
QK_DIM = QK_NOPE_DIM + QK_ROPE_DIM
V_HEAD_DIM = TOK_WIDTH // MLA_HEADS
Q_LORA_RANK = 384
KV_LORA_RANK = 256
ROPE_THETA = 10000.0
D_FF = 2816
EPS = 1e-6

LANES = 128
SUBLANES = 8
HEAD_PAD = LANES
BF16_SUBLANES = 2 * SUBLANES
V_ROWS = V_HEAD_DIM + BF16_SUBLANES
ROPE_HALF = QK_ROPE_DIM // 2
VMEM_LIMIT = 56 * 1024 * 1024
LOG2E = math.log2(math.e)


def _params(sem, vmem=VMEM_LIMIT):
    return pltpu.CompilerParams(dimension_semantics=sem, vmem_limit_bytes=vmem)


def _rms(x, g):
    ms = jnp.mean(x * x, axis=-1, keepdims=True)
    return x * lax.rsqrt(ms + EPS) * g


def _full(shape):
    zeros = (0,) * len(shape)
    return pl.BlockSpec(shape, lambda *_: zeros)


def _token_tile(t):
    for tm in (512, 256, 128, 64, 32, 16, 8):
        if t % tm == 0:
            return tm
    raise ValueError(f"token count {t} must be a multiple of 8")


def _rms_proj_kernel(x_ref, g_ref, w_ref, *o_refs, splits):
    h = _rms(x_ref[...], g_ref[...]).astype(BF16)
    y = jnp.dot(h, w_ref[...], preferred_element_type=F32)
    off = 0
    for o_ref, n in zip(o_refs, splits):
        o_ref[...] = y[:, off:off + n].astype(o_ref.dtype)
        off += n


def _rms_proj(x, g, w, splits, name):
    t, d = x.shape
    n = w.shape[1]
    assert sum(splits) == n and all(s % LANES == 0 for s in splits)
    tm = _token_tile(t)
    return pl.pallas_call(
        functools.partial(_rms_proj_kernel, splits=splits),
        grid=(t // tm,),
        in_specs=[pl.BlockSpec((tm, d), lambda i: (i, 0)), _full((1, d)), _full((d, n))],
        out_specs=[pl.BlockSpec((tm, s), lambda i: (i, 0)) for s in splits],
        out_shape=[jax.ShapeDtypeStruct((t, s), F32) for s in splits],
        compiler_params=_params(("parallel",)),
        name=name,
    )(x, g.reshape(1, d), w)


def _ffn_kernel(x_ref, g_ref, wgu_ref, wd_ref, o_ref):
    x = x_ref[...]
    h = _rms(x, g_ref[...]).astype(BF16)
    gu = jnp.dot(h, wgu_ref[...], preferred_element_type=F32)
    gate, up = gu[:, :D_FF], gu[:, D_FF:]
    act = (gate * jax.nn.sigmoid(gate) * up).astype(BF16)
    y = jnp.dot(act, wd_ref[...], preferred_element_type=F32)
    o_ref[...] = x + 0.5 * y


def _ffn(x, g, wgu, wd, name):
    t, d = x.shape
    tm = min(_token_tile(t), 256)
    once = pl.Buffered(1)
    return pl.pallas_call(
        _ffn_kernel,
        grid=(t // tm,),
        in_specs=[
            pl.BlockSpec((tm, d), lambda i: (i, 0)),
            _full((1, d)),
            pl.BlockSpec(wgu.shape, lambda i: (0, 0), pipeline_mode=once),
            pl.BlockSpec(wd.shape, lambda i: (0, 0), pipeline_mode=once),
        ],
        out_specs=pl.BlockSpec((tm, d), lambda i: (i, 0)),
        out_shape=jax.ShapeDtypeStruct((t, d), F32),
        compiler_params=_params(("parallel",)),
        name=name,
    )(x, g.reshape(1, d), wgu, wd)


def _head_rms_scale(x, head_dim):
    lane = lax.broadcasted_iota(jnp.int32, x.shape, 1)
    x2 = x * x
    scale = jnp.zeros_like(x)
    for h in range(x.shape[1] // head_dim):
        in_head = (lane >= h * head_dim) & (lane < (h + 1) * head_dim)
        ss = jnp.sum(jnp.where(in_head, x2, 0.0), axis=-1, keepdims=True)
        scale = jnp.where(in_head, lax.rsqrt(ss * (1.0 / head_dim) + EPS), scale)
    return scale


def _mem_kv_kernel(mem_ref, g_ref, w_ref, kg_ref, kt_ref, v_ref):
    mn = _rms(mem_ref[0], g_ref[...]).astype(BF16)
    kv = jnp.dot(mn, w_ref[...], preferred_element_type=F32)
    k, v = kv[:, :MEM_WIDTH], kv[:, MEM_WIDTH:]
    kn = k * _head_rms_scale(k, MEM_HEAD_DIM) * kg_ref[...]
    kt_ref[0] = kn.T.astype(BF16)
    v_ref[0] = v.astype(BF16)


def _mem_kv(mem, g, w, k_gain, name):
    b, m, d = mem.shape
    return pl.pallas_call(
        _mem_kv_kernel,
        grid=(b,),
        in_specs=[pl.BlockSpec((1, m, d), lambda i: (i, 0, 0)), _full((1, d)),
                  _full(w.shape), _full((1, MEM_WIDTH))],
        out_specs=[pl.BlockSpec((1, MEM_WIDTH, m), lambda i: (i, 0, 0)),
                   pl.BlockSpec((1, m, MEM_WIDTH), lambda i: (i, 0, 0))],
        out_shape=[jax.ShapeDtypeStruct((b, MEM_WIDTH, m), BF16),
                   jax.ShapeDtypeStruct((b, m, MEM_WIDTH), BF16)],
        compiler_params=_params(("parallel",)),
        name=name,
    )(mem, g.reshape(1, d), w, jnp.tile(k_gain, MEM_HEADS).reshape(1, MEM_WIDTH))


def _gelu_tanh(x):
    return 0.5 * x * (1.0 + jnp.tanh(math.sqrt(2.0 / math.pi) * (x + 0.044715 * (x * x * x))))


def _mem_attention(q, qg, kt, v):
    lane = lax.broadcasted_iota(jnp.int32, q.shape, 1)
    qn = q * _head_rms_scale(q, MEM_HEAD_DIM) * (qg * (MEM_HEAD_DIM ** -0.5))
    out = jnp.zeros_like(q)
    for h in range(MEM_HEADS):
        in_head = (lane >= h * MEM_HEAD_DIM) & (lane < (h + 1) * MEM_HEAD_DIM)
        qh = jnp.where(in_head, qn, 0.0).astype(BF16)
        s = jnp.dot(qh, kt, preferred_element_type=F32)
        p = jnp.exp(s - jnp.max(s, axis=-1, keepdims=True))
        denom = jnp.sum(p, axis=-1, keepdims=True)
        oh = jnp.dot(p.astype(BF16), v, preferred_element_type=F32)
        out = jnp.where(in_head, oh / denom, out)
    return out


def _mix_out_lru_kernel(x_ref, hf_ref, hr_ref, gate_ref, um_ref, qg_ref, kt_ref, v_ref,
                        wt_ref, wm_ref, o_ref):
    tok = (hf_ref[0] + hr_ref[0]) * _gelu_tanh(gate_ref[...])
    mem_out = _mem_attention(um_ref[...], qg_ref[...], kt_ref[0], v_ref[0])
    y = jnp.dot(tok.astype(BF16), wt_ref[...], preferred_element_type=F32)
    y = y + jnp.dot(mem_out.astype(BF16), wm_ref[...], preferred_element_type=F32)
    o_ref[...] = x_ref[...] + y


def _mix_out_mla_kernel(x_ref, tok_ref, um_ref, qg_ref, kt_ref, v_ref, wt_ref, wm_ref, o_ref):
    mem_out = _mem_attention(um_ref[...], qg_ref[...], kt_ref[0], v_ref[0])
    y = jnp.dot(tok_ref[...], wt_ref[...], preferred_element_type=F32)
    y = y + jnp.dot(mem_out.astype(BF16), wm_ref[...], preferred_element_type=F32)
    o_ref[...] = x_ref[...] + y


def _mix_out(x, toks, u_mem, q_gain, kt, v, w_out, seq, name):
    t, d = x.shape
    tm = _token_tile(seq)
    per_batch = seq // tm
    row = lambda w: pl.BlockSpec((tm, w), lambda i: (i, 0))
    per_b = lambda shape: pl.BlockSpec((1,) + shape, lambda i: (i // per_batch, 0, 0))
    common_specs = [row(MEM_WIDTH), _full((1, MEM_WIDTH)), per_b(kt.shape[1:]), per_b(v.shape[1:]),
                    _full((TOK_WIDTH, d)), _full((MEM_WIDTH, d))]
    common_args = (u_mem, jnp.tile(q_gain, MEM_HEADS).reshape(1, MEM_WIDTH), kt, v,
                   w_out[:TOK_WIDTH], w_out[TOK_WIDTH:])
    if len(toks) == 2:
        h_dirs, gate = toks
        kern = _mix_out_lru_kernel
        tok_specs = [pl.BlockSpec((1, tm, TOK_WIDTH), lambda i: (0, i, 0)),
                     pl.BlockSpec((1, tm, TOK_WIDTH), lambda i: (1, i, 0)), row(TOK_WIDTH)]
        tok_args = (h_dirs, h_dirs, gate)
    else:
        kern = _mix_out_mla_kernel
        tok_specs = [row(TOK_WIDTH)]
        tok_args = toks
    return pl.pallas_call(
        kern,
        grid=(t // tm,),
        in_specs=[row(d)] + tok_specs + common_specs,
        out_specs=row(d),
        out_shape=jax.ShapeDtypeStruct((t, d), F32),
        compiler_params=_params(("parallel",)),
        name=name,
    )(x, *tok_args, *common_args)


def _softplus(x):
    return jnp.maximum(x, 0.0) + jnp.log1p(jnp.exp(-jnp.abs(x)))


def _group_scan(a, b, reverse):
    row = lax.broadcasted_iota(jnp.int32, a.shape, 0)
    for dist in (1, 2, 4):
        if reverse:
            valid = row < SUBLANES - dist
            shift = SUBLANES - dist
        else:
            valid = row >= dist
            shift = dist
        a_sh = jnp.where(valid, pltpu.roll(a, shift, 0), 1.0)
        b_sh = jnp.where(valid, pltpu.roll(b, shift, 0), 0.0)
        b = a * b_sh + b
        a = a * a_sh
    return a, b


def _lru_kernel(cur_ref, prev_ref, next_ref, cw_ref, cb_ref, gw_ref, gb_ref, lam_ref, o_ref,
                ext_sc, a_sc, b_sc, h_sc, *, tc):
    direction = pl.program_id(0)
    step = pl.program_id(2)
    n_chunks = pl.num_programs(2)
    chunk = jnp.where(direction == 0, step, n_chunks - 1 - step)

    @pl.when(step == 0)
    def _():
        h_sc[...] = jnp.zeros_like(h_sc)

    ext_sc[0:SUBLANES, :] = jnp.where(chunk > 0, prev_ref[0], 0.0)
    ext_sc[SUBLANES:SUBLANES + tc, :] = cur_ref[0]
    ext_sc[SUBLANES + tc:, :] = jnp.where(chunk < n_chunks - 1, next_ref[0], 0.0)
    xc = jnp.broadcast_to(cb_ref[...], (tc, LRU_WIDTH))
    for tap in range(CONV_WIDTH):
        xc = xc + cw_ref[tap:tap + 1, :] * ext_sc[pl.ds(SUBLANES - CONV_LEFT + tap, tc), :]

    z = jnp.dot(xc.astype(BF16), gw_ref[0], preferred_element_type=F32)
    r_gate = jax.nn.sigmoid(z[:, :LRU_WIDTH] + gb_ref[0, 0:1, :])
    i_gate = jax.nn.sigmoid(z[:, LRU_WIDTH:] + gb_ref[0, 1:2, :])
    log_a = (-LRU_C) * r_gate * _softplus(-lam_ref[0])
    a = jnp.exp(log_a)
    a_sc[...] = a
    b_sc[...] = jnp.sqrt(-jnp.tanh(log_a) * (a * a + 1.0)) * (i_gate * xc)

    n_groups = tc // SUBLANES

    def run(reverse):
        def body(g, h):
            grp = n_groups - 1 - g if reverse else g
            off = pl.multiple_of(grp * SUBLANES, SUBLANES)
            a, b = _group_scan(a_sc[pl.ds(off, SUBLANES), :], b_sc[pl.ds(off, SUBLANES), :], reverse)
            hs = a * h + b
            o_ref[0, 0, pl.ds(off, SUBLANES), :] = hs
            last = hs[0:1, :] if reverse else hs[SUBLANES - 1:SUBLANES, :]
            return jnp.broadcast_to(last, hs.shape)
        h_sc[...] = lax.fori_loop(0, n_groups, body, h_sc[...])

    @pl.when(direction == 0)
    def _():
        run(False)

    @pl.when(direction == 1)
    def _():
        run(True)


def _lru_scan(xr, conv_w, conv_b, gate_w, gate_b, lam, name):
    b, s, w = xr.shape
    tc = _token_tile(s)
    n_chunks = s // tc
    per_chunk = tc // SUBLANES
    n_rows8 = s // SUBLANES

    def chunk_of(d, i):
        return jnp.where(d == 0, i, n_chunks - 1 - i)

    return pl.pallas_call(
        functools.partial(_lru_kernel, tc=tc),
        grid=(2, b, n_chunks),
        in_specs=[
            pl.BlockSpec((1, tc, w), lambda d, bi, i: (bi, chunk_of(d, i), 0)),
            pl.BlockSpec((1, SUBLANES, w),
                         lambda d, bi, i: (bi, jnp.maximum(chunk_of(d, i) * per_chunk - 1, 0), 0)),
            pl.BlockSpec((1, SUBLANES, w),
                         lambda d, bi, i: (bi, jnp.minimum((chunk_of(d, i) + 1) * per_chunk, n_rows8 - 1), 0)),
            _full((CONV_WIDTH, w)),
            _full((1, w)),
            pl.BlockSpec((1, w, 2 * w), lambda d, bi, i: (d, 0, 0)),
            pl.BlockSpec((1, 2, w), lambda d, bi, i: (d, 0, 0)),
            pl.BlockSpec((1, 1, w), lambda d, bi, i: (d, 0, 0)),
        ],
        out_specs=pl.BlockSpec((1, 1, tc, w), lambda d, bi, i: (d, bi, chunk_of(d, i), 0)),
        out_shape=jax.ShapeDtypeStruct((2, b, s, w), F32),
        scratch_shapes=[pltpu.VMEM((tc + 2 * SUBLANES, w), F32), pltpu.VMEM((tc, w), F32),
                        pltpu.VMEM((tc, w), F32), pltpu.VMEM((SUBLANES, w), F32)],
        compiler_params=_params(("arbitrary", "arbitrary", "arbitrary")),
        name=name,
    )(xr, xr, xr, conv_w, conv_b.reshape(1, w), gate_w, gate_b, lam.reshape(2, 1, w))


def _dense_block_diag(blocks):
    nb, bw, _ = blocks.shape
    eye = jnp.eye(nb, dtype=blocks.dtype)
    return (eye[:, None, :, None] * blocks[:, :, None, :]).reshape(nb * bw, nb * bw)


def _rope_table_kernel(pos_ref, freq_ref, c_ref, sa_ref, sb_ref):
    ang = pos_ref[...].astype(F32) * freq_ref[...]
    lane = lax.broadcasted_iota(jnp.int32, ang.shape, 1)
    cos, sin = jnp.cos(ang), jnp.sin(ang)
    rope_lo = QK_NOPE_DIM
    rope_mid = QK_NOPE_DIM + ROPE_HALF
    c_ref[...] = jnp.where(lane < rope_lo, 1.0, jnp.where(lane < QK_DIM, cos, 0.0))
    sa_ref[...] = jnp.where(lane < rope_lo, 0.0, jnp.where(lane < rope_mid, -sin, 0.0))
    sb_ref[...] = jnp.where(lane < rope_mid, 0.0, jnp.where(lane < QK_DIM, sin, 0.0))


def _rope_tables(positions):
    t = positions.size
    tm = _token_tile(t)
    inv_freq = ROPE_THETA ** (-jnp.arange(ROPE_HALF, dtype=F32) * (2.0 / QK_ROPE_DIM))
    freq = jnp.zeros((HEAD_PAD,), F32).at[QK_NOPE_DIM:QK_DIM].set(jnp.tile(inv_freq, 2))
    spec = pl.BlockSpec((tm, HEAD_PAD), lambda i: (i, 0))
    return pl.pallas_call(
        _rope_table_kernel,
        grid=(t // tm,),
        in_specs=[pl.BlockSpec((tm, 1), lambda i: (i, 0)), _full((1, HEAD_PAD))],
        out_specs=[spec, spec, spec],
        out_shape=[jax.ShapeDtypeStruct((t, HEAD_PAD), F32)] * 3,
        compiler_params=_params(("parallel",)),
        name="rope_tables",
    )(positions.reshape(t, 1), freq.reshape(1, HEAD_PAD))


def _rope(x, c, sa, sb):
    return x * c + pltpu.roll(x, HEAD_PAD - ROPE_HALF, 1) * sa + pltpu.roll(x, ROPE_HALF, 1) * sb


def _mla_prep_kernel(cq_ref, ckv_ref, kr_ref, c_ref, sa_ref, sb_ref, qa_ref, kva_ref, qg_ref, kg_ref,
                     wuq_ref, wukv_ref, qt_ref, k_ref, vt_ref):
    yq = jnp.dot(_rms(cq_ref[...], qa_ref[...]).astype(BF16), wuq_ref[...], preferred_element_type=F32)
    ykv = jnp.dot(_rms(ckv_ref[...], kva_ref[...]).astype(BF16), wukv_ref[...], preferred_element_type=F32)
    k_rope = kr_ref[...]
    c, sa, sb = c_ref[...], sa_ref[...], sb_ref[...]
    q_gain = qg_ref[...] * ((QK_DIM ** -0.5) * LOG2E)
    k_gain = kg_ref[...]
    lane = lax.broadcasted_iota(jnp.int32, k_rope.shape, 1)
    inv_dim = 1.0 / QK_DIM
    pad_row = lax.broadcasted_iota(jnp.int32, (BF16_SUBLANES, k_rope.shape[0]), 0)
    ones_rows = jnp.where(pad_row == 0, 1.0, 0.0)
    for h in range(MLA_HEADS):
        cols = slice(h * HEAD_PAD, (h + 1) * HEAD_PAD)
        qh = yq[:, cols]
        qn = qh * lax.rsqrt(jnp.sum(qh * qh, axis=-1, keepdims=True) * inv_dim + EPS) * q_gain
        qt_ref[0, h] = _rope(qn, c, sa, sb).T.astype(BF16)
        kvh = ykv[:, cols]
        kf = jnp.where(lane < QK_NOPE_DIM, kvh, k_rope)
        kn = kf * lax.rsqrt(jnp.sum(kf * kf, axis=-1, keepdims=True) * inv_dim + EPS) * k_gain
        k_ref[0, :, cols] = _rope(kn, c, sa, sb).astype(BF16)
        vt_ref[0, h, 0] = jnp.concatenate([kvh.T[QK_NOPE_DIM:, :], ones_rows], axis=0).astype(BF16)


def _mla_prep(cq, ckv, kr, tables, qa, kva, qg, kg, wuq, wukv, batch, seq, tk, name):
    n_kv = seq // tk
    row = lambda w: pl.BlockSpec((tk, w), lambda b, i: (b * n_kv + i, 0))
    width = MLA_HEADS * HEAD_PAD
    pad = lambda g: jnp.pad(g, (0, HEAD_PAD - QK_DIM)).reshape(1, HEAD_PAD)
    return pl.pallas_call(
        _mla_prep_kernel,
        grid=(batch, n_kv),
        in_specs=[row(Q_LORA_RANK), row(KV_LORA_RANK), row(HEAD_PAD), row(HEAD_PAD), row(HEAD_PAD),
                  row(HEAD_PAD), _full((1, Q_LORA_RANK)), _full((1, KV_LORA_RANK)),
                  _full((1, HEAD_PAD)), _full((1, HEAD_PAD)), _full(wuq.shape), _full(wukv.shape)],
        out_specs=[pl.BlockSpec((1, MLA_HEADS, HEAD_PAD, tk), lambda b, i: (b, 0, 0, i)),
                   pl.BlockSpec((1, tk, width), lambda b, i: (b, i, 0)),
                   pl.BlockSpec((1, MLA_HEADS, 1, V_ROWS, tk), lambda b, i: (b, 0, i, 0, 0))],
        out_shape=[jax.ShapeDtypeStruct((batch, MLA_HEADS, HEAD_PAD, seq), BF16),
                   jax.ShapeDtypeStruct((batch, seq, width), BF16),
                   jax.ShapeDtypeStruct((batch, MLA_HEADS, n_kv, V_ROWS, tk), BF16)],
        compiler_params=_params(("parallel", "parallel")),
        name=name,
    )(cq, ckv, kr, *tables, qa.reshape(1, -1), kva.reshape(1, -1), pad(qg), pad(kg), wuq, wukv)


def _attn_kernel(qt_ref, k_ref, vt_ref, o_ref, s_sc, acc_sc, *, tk, n_kv):
    tq = qt_ref.shape[-1]

    def scores(j, slot):
        off = pl.multiple_of(j * tk, tk)
        col_max = []
        for hh in range(2):
            k = k_ref[0, pl.ds(off, tk), hh * HEAD_PAD:(hh + 1) * HEAD_PAD]
            s_t = jnp.dot(k, qt_ref[0, hh], preferred_element_type=F32)
            s_sc[slot, hh] = s_t
            col_max.append(jnp.max(s_t, axis=0, keepdims=True))
        return tuple(col_max)

    def consume(j, slot, col_max, m_run):
        new = []
        for hh in range(2):
            m_new = jnp.maximum(m_run[hh], col_max[hh])
            alpha = jnp.exp2(m_run[hh] - m_new)
            p = jnp.exp2(s_sc[slot, hh] - m_new)
            pv = jnp.dot(vt_ref[0, hh, j], p.astype(BF16), preferred_element_type=F32)
            acc_sc[hh] = alpha * acc_sc[hh] + pv
            new.append(m_new)
        return tuple(new)

    acc_sc[...] = jnp.zeros_like(acc_sc)
    state = tuple(jnp.full((1, tq), -jnp.inf, F32) for _ in range(2))
    max_even = scores(0, 0)

    def pair(jj, carry):
        max_even, state = carry
        j = 2 * jj
        max_odd = scores(j + 1, 1)
        state = consume(j, 0, max_even, state)
        max_even = scores(j + 2, 0)
        state = consume(j + 1, 1, max_odd, state)
        return max_even, state

    max_even, state = lax.fori_loop(0, n_kv // 2 - 1, pair, (max_even, state))
    max_odd = scores(n_kv - 1, 1)
    state = consume(n_kv - 2, 0, max_even, state)
    state = consume(n_kv - 1, 1, max_odd, state)
    out_t = jnp.concatenate(
        [acc_sc[hh, :V_HEAD_DIM, :] / acc_sc[hh, V_HEAD_DIM:V_HEAD_DIM + 1, :] for hh in range(2)], axis=0)
    o_ref[0] = out_t.T.astype(o_ref.dtype)


def _attention(q_t, k, v_t, tq, name):
    batch, heads, _, seq = q_t.shape
    n_kv, tk = v_t.shape[2], v_t.shape[4]
    assert n_kv % 2 == 0, "the kv loop is unrolled in pairs of blocks"
    pairs = heads // 2
    return pl.pallas_call(
        functools.partial(_attn_kernel, tk=tk, n_kv=n_kv),
        grid=(batch, pairs, seq // tq),
        in_specs=[pl.BlockSpec((1, 2, HEAD_PAD, tq), lambda b, hp, qi: (b, hp, 0, qi)),
                  pl.BlockSpec((1, seq, 2 * HEAD_PAD), lambda b, hp, qi: (b, 0, hp)),
                  pl.BlockSpec((1, 2, n_kv, V_ROWS, tk), lambda b, hp, qi: (b, hp, 0, 0, 0))],
        out_specs=pl.BlockSpec((1, tq, 2 * V_HEAD_DIM), lambda b, hp, qi: (b, qi, hp)),
        out_shape=jax.ShapeDtypeStruct((batch, seq, heads * V_HEAD_DIM), BF16),
        scratch_shapes=[pltpu.VMEM((2, 2, tk, tq), F32), pltpu.VMEM((2, V_ROWS, tq), F32)],
        compiler_params=_params(("parallel", "parallel", "arbitrary")),
        name=name,
    )(q_t, k, v_t)


def _mla_in_weight(w_in):
    c_q = w_in[:, :Q_LORA_RANK]
    c_kv = w_in[:, Q_LORA_RANK:Q_LORA_RANK + KV_LORA_RANK]
    k_rope = w_in[:, Q_LORA_RANK + KV_LORA_RANK:Q_LORA_RANK + KV_LORA_RANK + QK_ROPE_DIM]
    u_mem = w_in[:, Q_LORA_RANK + KV_LORA_RANK + QK_ROPE_DIM:]
    k_rope = jnp.pad(k_rope, ((0, 0), (QK_NOPE_DIM, HEAD_PAD - QK_DIM)))
    return jnp.concatenate([c_q, c_kv, u_mem, k_rope], axis=1).astype(BF16)


def _pad_heads(w, head_dim):
    r = w.shape[0]
    w = w.reshape(r, MLA_HEADS, head_dim)
    return jnp.pad(w, ((0, 0), (0, 0), (0, HEAD_PAD - head_dim))).reshape(r, MLA_HEADS * HEAD_PAD).astype(BF16)


def kernel(x, mem, positions, ffn1_norm, ffn1_w_gate_up, ffn1_w_down, mix_norm, mem_norm, w_mem_kv,
           mem_q_norm, mem_k_norm, w_out, ffn2_norm, ffn2_w_gate_up, ffn2_w_down, lru_w_in, lru_conv_w,
           lru_conv_b, lru_gate_w, lru_gate_b, lru_lambda, mla_w_in, mla_q_a_norm, mla_w_uq,
           mla_kv_a_norm, mla_w_ukv, mla_q_norm, mla_k_norm):
    batch, seq, d = x.shape
    depth = ffn1_norm.shape[0]
    t = batch * seq
    xf = x.reshape(t, d)
    tables = _rope_tables(positions)
    tk = _token_tile(seq)

    for layer in range(depth):
        j = layer // 2
        xf = _ffn(xf, ffn1_norm[layer], ffn1_w_gate_up[layer].astype(BF16),
                  ffn1_w_down[layer].astype(BF16), f"ffn1_{layer}")
        kt, v = _mem_kv(mem, mem_norm[layer], w_mem_kv[layer].astype(BF16), mem_k_norm[layer],
                        f"mem_kv_{layer}")
        if layer % 2 == 0:
            gate, xr, u_mem = _rms_proj(xf, mix_norm[layer], lru_w_in[j].astype(BF16),
                                        (LRU_WIDTH, LRU_WIDTH, MEM_WIDTH), f"lru_in_{layer}")
            gw = jnp.stack([
                jnp.concatenate([_dense_block_diag(lru_gate_w[j, dr, 0]),
                                 _dense_block_diag(lru_gate_w[j, dr, 1])], axis=1)
                for dr in range(2)]).astype(BF16)
            h_dirs = _lru_scan(xr.reshape(batch, seq, LRU_WIDTH), lru_conv_w[j], lru_conv_b[j], gw,
                               lru_gate_b[j], lru_lambda[j], f"lru_scan_{layer}")
            toks = (h_dirs.reshape(2, t, LRU_WIDTH), gate)
        else:
            cq, ckv, u_mem, kr = _rms_proj(xf, mix_norm[layer], _mla_in_weight(mla_w_in[j]),
                                           (Q_LORA_RANK, KV_LORA_RANK, MEM_WIDTH, HEAD_PAD),
                                           f"mla_in_{layer}")
            q_t, k, v_t = _mla_prep(cq, ckv, kr, tables, mla_q_a_norm[j], mla_kv_a_norm[j],
                                    mla_q_norm[j], mla_k_norm[j], _pad_heads(mla_w_uq[j], QK_DIM),
                                    mla_w_ukv[j].astype(BF16), batch, seq, tk, f"mla_prep_{layer}")
            attn = _attention(q_t, k, v_t, tk, f"mla_attn_{layer}")
            toks = (attn.reshape(t, TOK_WIDTH),)
        xf = _mix_out(xf, toks, u_mem, mem_q_norm[layer], kt, v, w_out[layer].astype(BF16), seq,
                      f"mix_out_{layer}")
        xf = _ffn(xf, ffn2_norm[layer], ffn2_w_gate_up[layer].astype(BF16),
                  ffn2_w_down[layer].astype(BF16), f"ffn2_{layer}")
    return xf.reshape(batch, seq, d)
```

```python
import functools
import math

import jax
import jax.numpy as jnp
from jax import lax
from jax.experimental import pallas as pl
from jax.experimental.pallas import tpu as pltpu

F32 = jnp.float32
BF16 = jnp.bfloat16

D_MODEL = 1024
MEM_LEN = 256
MEM_HEADS = 4
MEM_HEAD_DIM = 64
MEM_WIDTH = MEM_HEADS * MEM_HEAD_DIM
TOK_WIDTH = D_MODEL - MEM_WIDTH
LRU_WIDTH = TOK_WIDTH
LRU_BLOCKS = 8
LRU_BLOCK_DIM = LRU_WIDTH // LRU_BLOCKS
CONV_WIDTH = 4
CONV_LEFT = 2
LRU_C = 8.0
MLA_HEADS = 12
QK_NOPE_DIM = 64
QK_ROPE_DIM = 32
QK_DIM = QK_NOPE_DIM + QK_ROPE_DIM
V_HEAD_DIM = TOK_WIDTH // MLA_HEADS
Q_LORA_RANK = 384
KV_LORA_RANK = 256
ROPE_THETA = 10000.0
D_FF = 2816
EPS = 1e-6

LANES = 128
SUBLANES = 8
HEAD_PAD = LANES
BF16_SUBLANES = 2 * SUBLANES
V_ROWS = V_HEAD_DIM + BF16_SUBLANES
ROPE_HALF = QK_ROPE_DIM // 2
VMEM_LIMIT = 56 * 1024 * 1024
LOG2E = math.log2(math.e)
ATTN_PAIRS_PER_TRIP = 5


def _params(sem, vmem=VMEM_LIMIT):
    return pltpu.CompilerParams(dimension_semantics=sem, vmem_limit_bytes=vmem)


def _rms(x, g):
    ms = jnp.mean(x * x, axis=-1, keepdims=True)
    return x * lax.rsqrt(ms + EPS) * g


def _full(shape):
    zeros = (0,) * len(shape)
    return pl.BlockSpec(shape, lambda *_: zeros)


def _token_tile(t):
    for tm in (512, 256, 128, 64, 32, 16, 8):
        if t % tm == 0:
            return tm
    raise ValueError(f"token count {t} must be a multiple of 8")


def _rms_proj_kernel(x_ref, g_ref, w_ref, *o_refs, splits):
    h = _rms(x_ref[...], g_ref[...]).astype(BF16)
    y = jnp.dot(h, w_ref[...], preferred_element_type=F32)
    off = 0
    for o_ref, n in zip(o_refs, splits):
        o_ref[...] = y[:, off:off + n].astype(o_ref.dtype)
        off += n


def _rms_proj(x, g, w, splits, name):
    t, d = x.shape
    n = w.shape[1]
    assert sum(splits) == n and all(s % LANES == 0 for s in splits)
    tm = _token_tile(t)
    return pl.pallas_call(
        functools.partial(_rms_proj_kernel, splits=splits),
        grid=(t // tm,),
        in_specs=[pl.BlockSpec((tm, d), lambda i: (i, 0)), _full((1, d)), _full((d, n))],
        out_specs=[pl.BlockSpec((tm, s), lambda i: (i, 0)) for s in splits],
        out_shape=[jax.ShapeDtypeStruct((t, s), F32) for s in splits],
        compiler_params=_params(("parallel",)),
        name=name,
    )(x, g.reshape(1, d), w)


def _ffn_kernel(x_ref, g_ref, wgu_ref, wd_ref, o_ref):
    x = x_ref[...]
    h = _rms(x, g_ref[...]).astype(BF16)
    gu = jnp.dot(h, wgu_ref[...], preferred_element_type=F32)
    gate, up = gu[:, :D_FF], gu[:, D_FF:]
    act = (gate * jax.nn.sigmoid(gate) * up).astype(BF16)
    y = jnp.dot(act, wd_ref[...], preferred_element_type=F32)
    o_ref[...] = x + 0.5 * y


def _ffn(x, g, wgu, wd, name):
    t, d = x.shape
    tm = _token_tile(t)
    once = pl.Buffered(1)
    return pl.pallas_call(
        _ffn_kernel,
        grid=(t // tm,),
        in_specs=[
            pl.BlockSpec((tm, d), lambda i: (i, 0)),
            _full((1, d)),
            pl.BlockSpec(wgu.shape, lambda i: (0, 0), pipeline_mode=once),
            pl.BlockSpec(wd.shape, lambda i: (0, 0), pipeline_mode=once),
        ],
        out_specs=pl.BlockSpec((tm, d), lambda i: (i, 0)),
        out_shape=jax.ShapeDtypeStruct((t, d), F32),
        compiler_params=_params(("parallel",)),
        name=name,
    )(x, g.reshape(1, d), wgu, wd)


def _head_rms_scale(x, head_dim):
    lane = lax.broadcasted_iota(jnp.int32, x.shape, 1)
    x2 = x * x
    scale = jnp.zeros_like(x)
    for h in range(x.shape[1] // head_dim):
        in_head = (lane >= h * head_dim) & (lane < (h + 1) * head_dim)
        ss = jnp.sum(jnp.where(in_head, x2, 0.0), axis=-1, keepdims=True)
        scale = jnp.where(in_head, lax.rsqrt(ss * (1.0 / head_dim) + EPS), scale)
    return scale


def _mem_kv_kernel(mem_ref, g_ref, w_ref, kg_ref, kt_ref, v_ref):
    mn = _rms(mem_ref[0], g_ref[...]).astype(BF16)
    kv = jnp.dot(mn, w_ref[...], preferred_element_type=F32)
    k, v = kv[:, :MEM_WIDTH], kv[:, MEM_WIDTH:]
    kn = k * _head_rms_scale(k, MEM_HEAD_DIM) * kg_ref[...]
    kt_ref[0] = kn.T.astype(BF16)
    v_ref[0] = v.astype(BF16)


def _mem_kv(mem, g, w, k_gain, name):
    b, m, d = mem.shape
    return pl.pallas_call(
        _mem_kv_kernel,
        grid=(b,),
        in_specs=[pl.BlockSpec((1, m, d), lambda i: (i, 0, 0)), _full((1, d)),
                  _full(w.shape), _full((1, MEM_WIDTH))],
        out_specs=[pl.BlockSpec((1, MEM_WIDTH, m), lambda i: (i, 0, 0)),
                   pl.BlockSpec((1, m, MEM_WIDTH), lambda i: (i, 0, 0))],
        out_shape=[jax.ShapeDtypeStruct((b, MEM_WIDTH, m), BF16),
                   jax.ShapeDtypeStruct((b, m, MEM_WIDTH), BF16)],
        compiler_params=_params(("parallel",)),
        name=name,
    )(mem, g.reshape(1, d), w, jnp.tile(k_gain, MEM_HEADS).reshape(1, MEM_WIDTH))


def _gelu_tanh(x):
    return 0.5 * x * (1.0 + jnp.tanh(math.sqrt(2.0 / math.pi) * (x + 0.044715 * (x * x * x))))


def _mem_attention(q, qg, kt, v):
    lane = lax.broadcasted_iota(jnp.int32, q.shape, 1)
    qn = q * _head_rms_scale(q, MEM_HEAD_DIM) * (qg * (MEM_HEAD_DIM ** -0.5))
    out = jnp.zeros_like(q)
    for h in range(MEM_HEADS):
        in_head = (lane >= h * MEM_HEAD_DIM) & (lane < (h + 1) * MEM_HEAD_DIM)
        qh = jnp.where(in_head, qn, 0.0).astype(BF16)
        s = jnp.dot(qh, kt, preferred_element_type=F32)
        p = jnp.exp(s - jnp.max(s, axis=-1, keepdims=True))
        denom = jnp.sum(p, axis=-1, keepdims=True)
        oh = jnp.dot(p.astype(BF16), v, preferred_element_type=F32)
        out = jnp.where(in_head, oh / denom, out)
    return out


def _mix_out_lru_kernel(x_ref, hf_ref, hr_ref, gate_ref, um_ref, qg_ref, kt_ref, v_ref,
                        wt_ref, wm_ref, o_ref):
    tok = (hf_ref[0] + hr_ref[0]) * _gelu_tanh(gate_ref[...])
    mem_out = _mem_attention(um_ref[...], qg_ref[...], kt_ref[0], v_ref[0])
    y = jnp.dot(tok.astype(BF16), wt_ref[...], preferred_element_type=F32)
    y = y + jnp.dot(mem_out.astype(BF16), wm_ref[...], preferred_element_type=F32)
    o_ref[...] = x_ref[...] + y


def _mix_out_mla_kernel(x_ref, tok_ref, um_ref, qg_ref, kt_ref, v_ref, wt_ref, wm_ref, o_ref):
    mem_out = _mem_attention(um_ref[...], qg_ref[...], kt_ref[0], v_ref[0])
    y = jnp.dot(tok_ref[...], wt_ref[...], preferred_element_type=F32)
    y = y + jnp.dot(mem_out.astype(BF16), wm_ref[...], preferred_element_type=F32)
    o_ref[...] = x_ref[...] + y


def _mix_out(x, toks, u_mem, q_gain, kt, v, w_out, seq, name):
    t, d = x.shape
    tm = _token_tile(seq)
    per_batch = seq // tm
    row = lambda w: pl.BlockSpec((tm, w), lambda i: (i, 0))
    per_b = lambda shape: pl.BlockSpec((1,) + shape, lambda i: (i // per_batch, 0, 0))
    common_specs = [row(MEM_WIDTH), _full((1, MEM_WIDTH)), per_b(kt.shape[1:]), per_b(v.shape[1:]),
                    _full((TOK_WIDTH, d)), _full((MEM_WIDTH, d))]
    common_args = (u_mem, jnp.tile(q_gain, MEM_HEADS).reshape(1, MEM_WIDTH), kt, v,
                   w_out[:TOK_WIDTH], w_out[TOK_WIDTH:])
    if len(toks) == 2:
        h_dirs, gate = toks
        kern = _mix_out_lru_kernel
        tok_specs = [pl.BlockSpec((1, tm, TOK_WIDTH), lambda i: (0, i, 0)),
                     pl.BlockSpec((1, tm, TOK_WIDTH), lambda i: (1, i, 0)), row(TOK_WIDTH)]
        tok_args = (h_dirs, h_dirs, gate)
    else:
        kern = _mix_out_mla_kernel
        tok_specs = [row(TOK_WIDTH)]
        tok_args = toks
    return pl.pallas_call(
        kern,
        grid=(t // tm,),
        in_specs=[row(d)] + tok_specs + common_specs,
        out_specs=row(d),
        out_shape=jax.ShapeDtypeStruct((t, d), F32),
        compiler_params=_params(("parallel",)),
        name=name,
    )(x, *tok_args, *common_args)


def _sigmoid(x):
    return 0.5 * jnp.tanh(0.5 * x) + 0.5


def _softplus(x):
    return jnp.maximum(x, 0.0) + jnp.log1p(jnp.exp(-jnp.abs(x)))


def _group_scan(a, b, reverse):
    row = lax.broadcasted_iota(jnp.int32, a.shape, 0)
    for dist in (1, 2, 4):
        if reverse:
            valid = row < SUBLANES - dist
            shift = SUBLANES - dist
        else:
            valid = row >= dist
            shift = dist
        a_sh = jnp.where(valid, pltpu.roll(a, shift, 0), 1.0)
        b_sh = jnp.where(valid, pltpu.roll(b, shift, 0), 0.0)
        b = a * b_sh + b
        a = a * a_sh
    return a, b


def _lru_kernel(cur_ref, prev_ref, next_ref, cw_ref, cb_ref, gw_ref, gb_ref, lam_ref, o_ref,
                ext_sc, a_sc, b_sc, h_sc, *, tc):
    direction = pl.program_id(0)
    step = pl.program_id(2)
    n_chunks = pl.num_programs(2)
    chunk = jnp.where(direction == 0, step, n_chunks - 1 - step)

    @pl.when(step == 0)
    def _():
        h_sc[...] = jnp.zeros_like(h_sc)

    ext_sc[0:SUBLANES, :] = jnp.where(chunk > 0, prev_ref[0], 0.0)
    ext_sc[SUBLANES:SUBLANES + tc, :] = cur_ref[0]
    ext_sc[SUBLANES + tc:, :] = jnp.where(chunk < n_chunks - 1, next_ref[0], 0.0)
    ext = ext_sc[...]
    xc = jnp.broadcast_to(cb_ref[...], (tc, LRU_WIDTH))
    for tap in range(CONV_WIDTH):
        shifted = pltpu.roll(ext, (CONV_LEFT - tap) % ext.shape[0], 0) if tap != CONV_LEFT else ext
        xc = xc + cw_ref[tap:tap + 1, :] * shifted[SUBLANES:SUBLANES + tc, :]

    z = jnp.dot(xc.astype(BF16), gw_ref[0], preferred_element_type=F32)
    r_gate = _sigmoid(z[:, :LRU_WIDTH] + gb_ref[0, 0:1, :])
    i_gate = _sigmoid(z[:, LRU_WIDTH:] + gb_ref[0, 1:2, :])
    log_a = (-LRU_C) * r_gate * _softplus(-lam_ref[0])
    a = jnp.exp(log_a)
    a_sc[...] = a
    b_sc[...] = jnp.sqrt(-jnp.tanh(log_a) * (a * a + 1.0)) * (i_gate * xc)

    n_groups = tc // SUBLANES

    def run(reverse):
        def body(g, h):
            grp = n_groups - 1 - g if reverse else g
            off = pl.multiple_of(grp * SUBLANES, SUBLANES)
            a, b = _group_scan(a_sc[pl.ds(off, SUBLANES), :], b_sc[pl.ds(off, SUBLANES), :], reverse)
            hs = a * h + b
            o_ref[0, 0, pl.ds(off, SUBLANES), :] = hs
            last = hs[0:1, :] if reverse else hs[SUBLANES - 1:SUBLANES, :]
            return jnp.broadcast_to(last, hs.shape)
        h_sc[...] = lax.fori_loop(0, n_groups, body, h_sc[...], unroll=4)

    @pl.when(direction == 0)
    def _():
        run(False)

    @pl.when(direction == 1)
    def _():
        run(True)


def _lru_scan(xr, conv_w, conv_b, gate_w, gate_b, lam, name):
    b, s, w = xr.shape
    tc = _token_tile(s)
    n_chunks = s // tc
    per_chunk = tc // SUBLANES
    n_rows8 = s // SUBLANES

    def chunk_of(d, i):
        return jnp.where(d == 0, i, n_chunks - 1 - i)

    return pl.pallas_call(
        functools.partial(_lru_kernel, tc=tc),
        grid=(2, b, n_chunks),
        in_specs=[
            pl.BlockSpec((1, tc, w), lambda d, bi, i: (bi, chunk_of(d, i), 0)),
            pl.BlockSpec((1, SUBLANES, w),
                         lambda d, bi, i: (bi, jnp.maximum(chunk_of(d, i) * per_chunk - 1, 0), 0)),
            pl.BlockSpec((1, SUBLANES, w),
                         lambda d, bi, i: (bi, jnp.minimum((chunk_of(d, i) + 1) * per_chunk, n_rows8 - 1), 0)),
            _full((CONV_WIDTH, w)),
            _full((1, w)),
            pl.BlockSpec((1, w, 2 * w), lambda d, bi, i: (d, 0, 0)),
            pl.BlockSpec((1, 2, w), lambda d, bi, i: (d, 0, 0)),
            pl.BlockSpec((1, 1, w), lambda d, bi, i: (d, 0, 0)),
        ],
        out_specs=pl.BlockSpec((1, 1, tc, w), lambda d, bi, i: (d, bi, chunk_of(d, i), 0)),
        out_shape=jax.ShapeDtypeStruct((2, b, s, w), F32),
        scratch_shapes=[pltpu.VMEM((tc + 2 * SUBLANES, w), F32), pltpu.VMEM((tc, w), F32),
                        pltpu.VMEM((tc, w), F32), pltpu.VMEM((SUBLANES, w), F32)],
        compiler_params=_params(("arbitrary", "arbitrary", "arbitrary")),
        name=name,
    )(xr, xr, xr, conv_w, conv_b.reshape(1, w), gate_w, gate_b, lam.reshape(2, 1, w))


def _dense_block_diag(blocks):
    nb, bw, _ = blocks.shape
    eye = jnp.eye(nb, dtype=blocks.dtype)
    return (eye[:, None, :, None] * blocks[:, :, None, :]).reshape(nb * bw, nb * bw)


ROPE_X1_LO = 0
ROPE_X2_LO = HEAD_PAD // 2
NOPE_A = (ROPE_HALF, HEAD_PAD // 2)
NOPE_B = (HEAD_PAD // 2 + ROPE_HALF, QK_DIM)
NOPE_SPLIT = NOPE_A[1] - NOPE_A[0]


def _slot_pad(nope, rope):
    ref = nope if nope is not None else rope
    lead = ref.shape[:-1]
    z = lambda n: jnp.zeros(lead + (n,), ref.dtype)
    nope = z(QK_NOPE_DIM) if nope is None else nope
    rope = z(QK_ROPE_DIM) if rope is None else rope
    return jnp.concatenate([rope[..., :ROPE_HALF], nope[..., :NOPE_SPLIT], rope[..., ROPE_HALF:],
                            nope[..., NOPE_SPLIT:], z(HEAD_PAD - QK_DIM)], axis=-1)


def _rope_table_kernel(pos_ref, freq_ref, cos_t_ref, sin_t_ref, c_ref, s_ref):
    ang = freq_ref[...] * pos_ref[...].astype(F32)
    cos_t, sin_t = jnp.cos(ang), jnp.sin(ang)
    cos_t_ref[...] = cos_t
    sin_t_ref[...] = sin_t
    rest = HEAD_PAD // 2 - ROPE_HALF
    ones = jnp.ones((rest, ang.shape[1]), F32)
    zeros = jnp.zeros((rest, ang.shape[1]), F32)
    c_ref[...] = jnp.concatenate([cos_t, ones, cos_t, ones], axis=0).T
    s_ref[...] = jnp.concatenate([-sin_t, zeros, sin_t, zeros], axis=0).T


def _rope_tables(positions):
    t = positions.size
    tm = _token_tile(t)
    inv_freq = ROPE_THETA ** (-jnp.arange(ROPE_HALF, dtype=F32) * (2.0 / QK_ROPE_DIM))
    col = pl.BlockSpec((ROPE_HALF, tm), lambda i: (0, i))
    row = pl.BlockSpec((tm, HEAD_PAD), lambda i: (i, 0))
    return pl.pallas_call(
        _rope_table_kernel,
        grid=(t // tm,),
        in_specs=[pl.BlockSpec((1, tm), lambda i: (0, i)), _full((ROPE_HALF, 1))],
        out_specs=[col, col, row, row],
        out_shape=[jax.ShapeDtypeStruct((ROPE_HALF, t), F32)] * 2 + [jax.ShapeDtypeStruct((t, HEAD_PAD), F32)] * 2,
        compiler_params=_params(("parallel",)),
        name="rope_tables",
    )(positions.reshape(1, t), inv_freq.reshape(ROPE_HALF, 1))


def _mla_prep_kernel(cq_ref, ckv_ref, kr_ref, cos_t_ref, sin_t_ref, c_ref, s_ref, qa_ref, kva_ref, qg_ref,
                     kg_ref, wq_t_ref, wk_ref, wv_t_ref, qt_ref, k_ref, vt_ref):
    tk = cq_ref.shape[0]
    cqn = _rms(cq_ref[...], qa_ref[...])
    ckn = _rms(ckv_ref[...], kva_ref[...])
    ckn_bf = ckn.astype(BF16)
    yq_t = jnp.dot(wq_t_ref[...], cqn.T.astype(BF16), preferred_element_type=F32)
    yv_t = jnp.dot(wv_t_ref[...], ckn.T.astype(BF16), preferred_element_type=F32)
    yk = jnp.dot(ckn_bf, wk_ref[...], preferred_element_type=F32)
    k_rope = kr_ref[...]
    cos_t, sin_t = cos_t_ref[...], sin_t_ref[...]
    c, s = c_ref[...], s_ref[...]
    q_gain = jnp.broadcast_to(qg_ref[...] * ((QK_DIM ** -0.5) * LOG2E), (HEAD_PAD, tk))
    k_gain = kg_ref[...]
    inv_dim = 1.0 / QK_DIM
    pad_row = lax.broadcasted_iota(jnp.int32, (BF16_SUBLANES, tk), 0)
    ones_rows = jnp.where(pad_row == 0, 1.0, 0.0)
    x1 = slice(ROPE_X1_LO, ROPE_X1_LO + ROPE_HALF)
    x2 = slice(ROPE_X2_LO, ROPE_X2_LO + ROPE_HALF)
    for h in range(MLA_HEADS):
        rows = slice(h * HEAD_PAD, (h + 1) * HEAD_PAD)
        qh = yq_t[rows, :]
        qn = qh * lax.rsqrt(jnp.sum(qh * qh, axis=0, keepdims=True) * inv_dim + EPS) * q_gain
        q1, q2 = qn[x1, :], qn[x2, :]
        qt_ref[0, h] = jnp.concatenate(
            [q1 * cos_t - q2 * sin_t, qn[NOPE_A[0]:NOPE_A[1], :], q2 * cos_t + q1 * sin_t, qn[x2.stop:, :]],
            axis=0).astype(BF16)
        kf = yk[:, rows] + k_rope
        kn = kf * lax.rsqrt(jnp.sum(kf * kf, axis=-1, keepdims=True) * inv_dim + EPS) * k_gain
        k_ref[0, :, rows] = (kn * c + pltpu.roll(kn, HEAD_PAD // 2, 1) * s).astype(BF16)
        vt_ref[0, h, 0] = jnp.concatenate(
            [yv_t[h * V_HEAD_DIM:(h + 1) * V_HEAD_DIM, :], ones_rows], axis=0).astype(BF16)


def _mla_prep(cq, ckv, kr, tables, qa, kva, qg, kg, w_uq, w_ukv, batch, seq, tk, name):
    n_kv = seq // tk
    row = lambda w: pl.BlockSpec((tk, w), lambda b, i: (b * n_kv + i, 0))
    col = pl.BlockSpec((ROPE_HALF, tk), lambda b, i: (0, b * n_kv + i))
    width = MLA_HEADS * HEAD_PAD
    wq = w_uq.reshape(Q_LORA_RANK, MLA_HEADS, QK_DIM)
    wq_t = _slot_pad(wq[..., :QK_NOPE_DIM], wq[..., QK_NOPE_DIM:]).reshape(Q_LORA_RANK, width).T.astype(BF16)
    wkv = w_ukv.reshape(KV_LORA_RANK, MLA_HEADS, QK_NOPE_DIM + V_HEAD_DIM)
    wk = _slot_pad(wkv[..., :QK_NOPE_DIM], None).reshape(KV_LORA_RANK, width).astype(BF16)
    wv_t = wkv[..., QK_NOPE_DIM:].reshape(KV_LORA_RANK, MLA_HEADS * V_HEAD_DIM).T.astype(BF16)
    qg_col = _slot_pad(qg[:QK_NOPE_DIM], qg[QK_NOPE_DIM:]).reshape(HEAD_PAD, 1)
    kg_row = _slot_pad(kg[:QK_NOPE_DIM], kg[QK_NOPE_DIM:]).reshape(1, HEAD_PAD)
    cos_t, sin_t, c, s = tables
    return pl.pallas_call(
        _mla_prep_kernel,
        grid=(batch, n_kv),
        in_specs=[row(Q_LORA_RANK), row(KV_LORA_RANK), row(HEAD_PAD), col, col, row(HEAD_PAD), row(HEAD_PAD),
                  _full((1, Q_LORA_RANK)), _full((1, KV_LORA_RANK)), _full((HEAD_PAD, 1)), _full((1, HEAD_PAD)),
                  _full(wq_t.shape), _full(wk.shape), _full(wv_t.shape)],
        out_specs=[pl.BlockSpec((1, MLA_HEADS, HEAD_PAD, tk), lambda b, i: (b, 0, 0, i)),
                   pl.BlockSpec((1, tk, width), lambda b, i: (b, i, 0)),
                   pl.BlockSpec((1, MLA_HEADS, 1, V_ROWS, tk), lambda b, i: (b, 0, i, 0, 0))],
        out_shape=[jax.ShapeDtypeStruct((batch, MLA_HEADS, HEAD_PAD, seq), BF16),
                   jax.ShapeDtypeStruct((batch, seq, width), BF16),
                   jax.ShapeDtypeStruct((batch, MLA_HEADS, n_kv, V_ROWS, tk), BF16)],
        compiler_params=_params(("parallel", "parallel")),
        name=name,
    )(cq, ckv, kr, cos_t, sin_t, c, s, qa.reshape(1, -1), kva.reshape(1, -1), qg_col, kg_row, wq_t, wk, wv_t)


def _attn_kernel(qt_ref, k_ref, vt_ref, o_ref, s_sc, acc_sc, *, tk, n_kv, pairs_per_trip):
    tq = qt_ref.shape[-1]

    def scores(j, slot):
        off = pl.multiple_of(j * tk, tk)
        col_max = []
        for hh in range(2):
            k = k_ref[0, pl.ds(off, tk), hh * HEAD_PAD:(hh + 1) * HEAD_PAD]
            s_t = jnp.dot(k, qt_ref[0, hh], preferred_element_type=F32)
            s_sc[slot, hh] = s_t
            col_max.append(jnp.max(s_t, axis=0, keepdims=True))
        return tuple(col_max)

    def consume(j, slot, col_max, m_run):
        new = []
        for hh in range(2):
            m_new = jnp.maximum(m_run[hh], col_max[hh])
            alpha = jnp.exp2(m_run[hh] - m_new)
            p = jnp.exp2(s_sc[slot, hh] - m_new)
            pv = jnp.dot(vt_ref[0, hh, j], p.astype(BF16), preferred_element_type=F32)
            acc_sc[hh] = alpha * acc_sc[hh] + pv
            new.append(m_new)
        return tuple(new)

    acc_sc[...] = jnp.zeros_like(acc_sc)
    state = tuple(jnp.full((1, tq), -jnp.inf, F32) for _ in range(2))
    max_even = scores(0, 0)

    def pair(j, max_even, state):
        max_odd = scores(j + 1, 1)
        state = consume(j, 0, max_even, state)
        max_even = scores(j + 2, 0)
        state = consume(j + 1, 1, max_odd, state)
        return max_even, state

    def body(it, carry):
        for u in range(pairs_per_trip):
            carry = pair(2 * (it * pairs_per_trip + u), *carry)
        return carry

    n_full_pairs = n_kv // 2 - 1
    n_trips = n_full_pairs // pairs_per_trip
    max_even, state = lax.fori_loop(0, n_trips, body, (max_even, state))
    for p in range(n_trips * pairs_per_trip, n_full_pairs):
        max_even, state = pair(2 * p, max_even, state)
    max_odd = scores(n_kv - 1, 1)
    state = consume(n_kv - 2, 0, max_even, state)
    state = consume(n_kv - 1, 1, max_odd, state)
    out_t = jnp.concatenate(
        [acc_sc[hh, :V_HEAD_DIM, :] / acc_sc[hh, V_HEAD_DIM:V_HEAD_DIM + 1, :] for hh in range(2)], axis=0)
    o_ref[0] = out_t.T.astype(o_ref.dtype)


def _attention(q_t, k, v_t, tq, name):
    batch, heads, _, seq = q_t.shape
    n_kv, tk = v_t.shape[2], v_t.shape[4]
    assert n_kv % 2 == 0, "the kv loop is unrolled in pairs of blocks"
    pairs = heads // 2
    return pl.pallas_call(
        functools.partial(_attn_kernel, tk=tk, n_kv=n_kv, pairs_per_trip=ATTN_PAIRS_PER_TRIP),
        grid=(batch, pairs, seq // tq),
        in_specs=[pl.BlockSpec((1, 2, HEAD_PAD, tq), lambda b, hp, qi: (b, hp, 0, qi)),
                  pl.BlockSpec((1, seq, 2 * HEAD_PAD), lambda b, hp, qi: (b, 0, hp)),
                  pl.BlockSpec((1, 2, n_kv, V_ROWS, tk), lambda b, hp, qi: (b, hp, 0, 0, 0))],
        out_specs=pl.BlockSpec((1, tq, 2 * V_HEAD_DIM), lambda b, hp, qi: (b, qi, hp)),
        out_shape=jax.ShapeDtypeStruct((batch, seq, heads * V_HEAD_DIM), BF16),
        scratch_shapes=[pltpu.VMEM((2, 2, tk, tq), F32), pltpu.VMEM((2, V_ROWS, tq), F32)],
        compiler_params=_params(("parallel", "parallel", "arbitrary")),
        name=name,
    )(q_t, k, v_t)


def _mla_in_weight(w_in):
    c_q = w_in[:, :Q_LORA_RANK]
    c_kv = w_in[:, Q_LORA_RANK:Q_LORA_RANK + KV_LORA_RANK]
    k_rope = w_in[:, Q_LORA_RANK + KV_LORA_RANK:Q_LORA_RANK + KV_LORA_RANK + QK_ROPE_DIM]
    u_mem = w_in[:, Q_LORA_RANK + KV_LORA_RANK + QK_ROPE_DIM:]
    return jnp.concatenate([c_q, c_kv, u_mem, _slot_pad(None, k_rope)], axis=1).astype(BF16)


def kernel(x, mem, positions, ffn1_norm, ffn1_w_gate_up, ffn1_w_down, mix_norm, mem_norm, w_mem_kv,
           mem_q_norm, mem_k_norm, w_out, ffn2_norm, ffn2_w_gate_up, ffn2_w_down, lru_w_in, lru_conv_w,
           lru_conv_b, lru_gate_w, lru_gate_b, lru_lambda, mla_w_in, mla_q_a_norm, mla_w_uq,
           mla_kv_a_norm, mla_w_ukv, mla_q_norm, mla_k_norm):
    batch, seq, d = x.shape
    depth = ffn1_norm.shape[0]
    t = batch * seq
    xf = x.reshape(t, d)
    tables = _rope_tables(positions)
    tk = _token_tile(seq)

    for layer in range(depth):
        j = layer // 2
        xf = _ffn(xf, ffn1_norm[layer], ffn1_w_gate_up[layer].astype(BF16),
                  ffn1_w_down[layer].astype(BF16), f"ffn1_{layer}")
        kt, v = _mem_kv(mem, mem_norm[layer], w_mem_kv[layer].astype(BF16), mem_k_norm[layer],
                        f"mem_kv_{layer}")
        if layer % 2 == 0:
            gate, xr, u_mem = _rms_proj(xf, mix_norm[layer], lru_w_in[j].astype(BF16),
                                        (LRU_WIDTH, LRU_WIDTH, MEM_WIDTH), f"lru_in_{layer}")
            gw = jnp.stack([
                jnp.concatenate([_dense_block_diag(lru_gate_w[j, dr, 0]),
                                 _dense_block_diag(lru_gate_w[j, dr, 1])], axis=1)
                for dr in range(2)]).astype(BF16)
            h_dirs = _lru_scan(xr.reshape(batch, seq, LRU_WIDTH), lru_conv_w[j], lru_conv_b[j], gw,
                               lru_gate_b[j], lru_lambda[j], f"lru_scan_{layer}")
            toks = (h_dirs.reshape(2, t, LRU_WIDTH), gate)
        else:
            cq, ckv, u_mem, kr = _rms_proj(xf, mix_norm[layer], _mla_in_weight(mla_w_in[j]),
                                           (Q_LORA_RANK, KV_LORA_RANK, MEM_WIDTH, HEAD_PAD),
                                           f"mla_in_{layer}")
            q_t, k, v_t = _mla_prep(cq, ckv, kr, tables, mla_q_a_norm[j], mla_kv_a_norm[j],
                                    mla_q_norm[j], mla_k_norm[j], mla_w_uq[j], mla_w_ukv[j],
                                    batch, seq, tk, f"mla_prep_{layer}")
            attn = _attention(q_t, k, v_t, tk, f"mla_attn_{layer}")
            toks = (attn.reshape(t, TOK_WIDTH),)
        xf = _mix_out(xf, toks, u_mem, mem_q_norm[layer], kt, v, w_out[layer].astype(BF16), seq,
                      f"mix_out_{layer}")
        xf = _ffn(xf, ffn2_norm[layer], ffn2_w_gate_up[layer].astype(BF16),
                  ffn2_w_down[layer].astype(BF16), f"ffn2_{layer}")
    return xf.reshape(batch, seq, d)
```

```python
import functools
import math

import jax
import jax.numpy as jnp
from jax import lax
from jax.experimental import pallas as pl
from jax.experimental.pallas import tpu as pltpu

F32 = jnp.float32
BF16 = jnp.bfloat16

D_MODEL = 1024
MEM_LEN = 256
MEM_HEADS = 4
MEM_HEAD_DIM = 64
MEM_WIDTH = MEM_HEADS * MEM_HEAD_DIM
TOK_WIDTH = D_MODEL - MEM_WIDTH
LRU_WIDTH = TOK_WIDTH
LRU_BLOCKS = 8
LRU_BLOCK_DIM = LRU_WIDTH // LRU_BLOCKS
CONV_WIDTH = 4
CONV_LEFT = 2
LRU_C = 8.0
MLA_HEADS = 12
QK_NOPE_DIM = 64
QK_ROPE_DIM = 32
QK_DIM = QK_NOPE_DIM + QK_ROPE_DIM
V_HEAD_DIM = TOK_WIDTH // MLA_HEADS
Q_LORA_RANK = 384
KV_LORA_RANK = 256
ROPE_THETA = 10000.0
D_FF = 2816
EPS = 1e-6

LANES = 128
SUBLANES = 8
HEAD_PAD = LANES
BF16_SUBLANES = 2 * SUBLANES
V_ROWS = V_HEAD_DIM + BF16_SUBLANES
ROPE_HALF = QK_ROPE_DIM // 2
VMEM_LIMIT = 56 * 1024 * 1024
LOG2E = math.log2(math.e)
ATTN_BLOCKS_PER_TRIP = 8
LAG_LIMIT = 64.0
ATTN_PAIRS_PER_TRIP = 1


def _params(sem, vmem=VMEM_LIMIT):
    return pltpu.CompilerParams(dimension_semantics=sem, vmem_limit_bytes=vmem)


def _rms(x, g):
    ms = jnp.mean(x * x, axis=-1, keepdims=True)
    return x * lax.rsqrt(ms + EPS) * g


def _full(shape):
    zeros = (0,) * len(shape)
    return pl.BlockSpec(shape, lambda *_: zeros)


def _token_tile(t):
    for tm in (512, 256, 128, 64, 32, 16, 8):
        if t % tm == 0:
            return tm
    raise ValueError(f"token count {t} must be a multiple of 8")


def _rms_proj_kernel(x_ref, g_ref, w_ref, *o_refs, splits):
    h = _rms(x_ref[...], g_ref[...]).astype(BF16)
    y = jnp.dot(h, w_ref[...], preferred_element_type=F32)
    off = 0
    for o_ref, n in zip(o_refs, splits):
        o_ref[...] = y[:, off:off + n].astype(o_ref.dtype)
        off += n


def _rms_proj(x, g, w, splits, name):
    t, d = x.shape
    n = w.shape[1]
    assert sum(splits) == n and all(s % LANES == 0 for s in splits)
    tm = _token_tile(t)
    return pl.pallas_call(
        functools.partial(_rms_proj_kernel, splits=splits),
        grid=(t // tm,),
        in_specs=[pl.BlockSpec((tm, d), lambda i: (i, 0)), _full((1, d)), _full((d, n))],
        out_specs=[pl.BlockSpec((tm, s), lambda i: (i, 0)) for s in splits],
        out_shape=[jax.ShapeDtypeStruct((t, s), F32) for s in splits],
        compiler_params=_params(("parallel",)),
        name=name,
    )(x, g.reshape(1, d), w)


def _ffn_kernel(x_ref, g_ref, wgu_ref, wd_ref, o_ref):
    x = x_ref[...]
    h = _rms(x, g_ref[...]).astype(BF16)
    gu = jnp.dot(h, wgu_ref[...], preferred_element_type=F32)
    gate, up = gu[:, :D_FF], gu[:, D_FF:]
    act = (gate * jax.nn.sigmoid(gate) * up).astype(BF16)
    y = jnp.dot(act, wd_ref[...], preferred_element_type=F32)
    o_ref[...] = x + 0.5 * y


def _ffn(x, g, wgu, wd, name):
    t, d = x.shape
    tm = _token_tile(t)
    once = pl.Buffered(1)
    return pl.pallas_call(
        _ffn_kernel,
        grid=(t // tm,),
        in_specs=[
            pl.BlockSpec((tm, d), lambda i: (i, 0)),
            _full((1, d)),
            pl.BlockSpec(wgu.shape, lambda i: (0, 0), pipeline_mode=once),
            pl.BlockSpec(wd.shape, lambda i: (0, 0), pipeline_mode=once),
        ],
        out_specs=pl.BlockSpec((tm, d), lambda i: (i, 0)),
        out_shape=jax.ShapeDtypeStruct((t, d), F32),
        compiler_params=_params(("parallel",)),
        name=name,
    )(x, g.reshape(1, d), wgu, wd)


def _head_rms_scale(x, head_dim):
    lane = lax.broadcasted_iota(jnp.int32, x.shape, 1)
    x2 = x * x
    scale = jnp.zeros_like(x)
    for h in range(x.shape[1] // head_dim):
        in_head = (lane >= h * head_dim) & (lane < (h + 1) * head_dim)
        ss = jnp.sum(jnp.where(in_head, x2, 0.0), axis=-1, keepdims=True)
        scale = jnp.where(in_head, lax.rsqrt(ss * (1.0 / head_dim) + EPS), scale)
    return scale


def _mem_kv_kernel(mem_ref, g_ref, w_ref, kg_ref, kt_ref, v_ref):
    mn = _rms(mem_ref[0], g_ref[...]).astype(BF16)
    kv = jnp.dot(mn, w_ref[...], preferred_element_type=F32)
    k, v = kv[:, :MEM_WIDTH], kv[:, MEM_WIDTH:]
    kn = k * _head_rms_scale(k, MEM_HEAD_DIM) * kg_ref[...]
    kt_ref[0] = kn.T.astype(BF16)
    v_ref[0] = v.astype(BF16)


def _mem_kv(mem, g, w, k_gain, name):
    b, m, d = mem.shape
    return pl.pallas_call(
        _mem_kv_kernel,
        grid=(b,),
        in_specs=[pl.BlockSpec((1, m, d), lambda i: (i, 0, 0)), _full((1, d)),
                  _full(w.shape), _full((1, MEM_WIDTH))],
        out_specs=[pl.BlockSpec((1, MEM_WIDTH, m), lambda i: (i, 0, 0)),
                   pl.BlockSpec((1, m, MEM_WIDTH), lambda i: (i, 0, 0))],
        out_shape=[jax.ShapeDtypeStruct((b, MEM_WIDTH, m), BF16),
                   jax.ShapeDtypeStruct((b, m, MEM_WIDTH), BF16)],
        compiler_params=_params(("parallel",)),
        name=name,
    )(mem, g.reshape(1, d), w, jnp.tile(k_gain, MEM_HEADS).reshape(1, MEM_WIDTH))


def _gelu_tanh(x):
    return 0.5 * x * (1.0 + jnp.tanh(math.sqrt(2.0 / math.pi) * (x + 0.044715 * (x * x * x))))


def _mem_attention(q, qg, kt, v):
    lane = lax.broadcasted_iota(jnp.int32, q.shape, 1)
    qn = q * _head_rms_scale(q, MEM_HEAD_DIM) * (qg * (MEM_HEAD_DIM ** -0.5))
    out = jnp.zeros_like(q)
    for h in range(MEM_HEADS):
        in_head = (lane >= h * MEM_HEAD_DIM) & (lane < (h + 1) * MEM_HEAD_DIM)
        qh = jnp.where(in_head, qn, 0.0).astype(BF16)
        s = jnp.dot(qh, kt, preferred_element_type=F32)
        p = jnp.exp(s - jnp.max(s, axis=-1, keepdims=True))
        denom = jnp.sum(p, axis=-1, keepdims=True)
        oh = jnp.dot(p.astype(BF16), v, preferred_element_type=F32)
        out = jnp.where(in_head, oh / denom, out)
    return out


def _mix_out_lru_kernel(x_ref, hf_ref, hr_ref, gate_ref, um_ref, qg_ref, kt_ref, v_ref,
                        wt_ref, wm_ref, o_ref):
    tok = (hf_ref[0] + hr_ref[0]) * _gelu_tanh(gate_ref[...])
    mem_out = _mem_attention(um_ref[...], qg_ref[...], kt_ref[0], v_ref[0])
    y = jnp.dot(tok.astype(BF16), wt_ref[...], preferred_element_type=F32)
    y = y + jnp.dot(mem_out.astype(BF16), wm_ref[...], preferred_element_type=F32)
    o_ref[...] = x_ref[...] + y


def _mix_out_mla_kernel(x_ref, tok_ref, um_ref, qg_ref, kt_ref, v_ref, wt_ref, wm_ref, o_ref):
    mem_out = _mem_attention(um_ref[...], qg_ref[...], kt_ref[0], v_ref[0])
    y = jnp.dot(tok_ref[...], wt_ref[...], preferred_element_type=F32)
    y = y + jnp.dot(mem_out.astype(BF16), wm_ref[...], preferred_element_type=F32)
    o_ref[...] = x_ref[...] + y


def _mix_out(x, toks, u_mem, q_gain, kt, v, w_out, seq, name):
    t, d = x.shape
    tm = _token_tile(seq)
    per_batch = seq // tm
    row = lambda w: pl.BlockSpec((tm, w), lambda i: (i, 0))
    per_b = lambda shape: pl.BlockSpec((1,) + shape, lambda i: (i // per_batch, 0, 0))
    common_specs = [row(MEM_WIDTH), _full((1, MEM_WIDTH)), per_b(kt.shape[1:]), per_b(v.shape[1:]),
                    _full((TOK_WIDTH, d)), _full((MEM_WIDTH, d))]
    common_args = (u_mem, jnp.tile(q_gain, MEM_HEADS).reshape(1, MEM_WIDTH), kt, v,
                   w_out[:TOK_WIDTH], w_out[TOK_WIDTH:])
    if len(toks) == 2:
        h_dirs, gate = toks
        kern = _mix_out_lru_kernel
        tok_specs = [pl.BlockSpec((1, tm, TOK_WIDTH), lambda i: (0, i, 0)),
                     pl.BlockSpec((1, tm, TOK_WIDTH), lambda i: (1, i, 0)), row(TOK_WIDTH)]
        tok_args = (h_dirs, h_dirs, gate)
    else:
        kern = _mix_out_mla_kernel
        tok_specs = [row(TOK_WIDTH)]
        tok_args = toks
    return pl.pallas_call(
        kern,
        grid=(t // tm,),
        in_specs=[row(d)] + tok_specs + common_specs,
        out_specs=row(d),
        out_shape=jax.ShapeDtypeStruct((t, d), F32),
        compiler_params=_params(("parallel",)),
        name=name,
    )(x, *tok_args, *common_args)


def _sigmoid(x):
    return 0.5 * jnp.tanh(0.5 * x) + 0.5


def _softplus(x):
    return jnp.maximum(x, 0.0) + jnp.log1p(jnp.exp(-jnp.abs(x)))


def _group_scan(a, b, reverse):
    row = lax.broadcasted_iota(jnp.int32, a.shape, 0)
    for dist in (1, 2, 4):
        if reverse:
            valid = row < SUBLANES - dist
            shift = SUBLANES - dist
        else:
            valid = row >= dist
            shift = dist
        a_sh = jnp.where(valid, pltpu.roll(a, shift, 0), 1.0)
        b_sh = jnp.where(valid, pltpu.roll(b, shift, 0), 0.0)
        b = a * b_sh + b
        a = a * a_sh
    return a, b


def _lru_kernel(cur_ref, prev_ref, next_ref, cw_ref, cb_ref, gw_ref, gb_ref, lam_ref, o_ref,
                ext_sc, a_sc, b_sc, h_sc, *, tc):
    direction = pl.program_id(0)
    step = pl.program_id(2)
    n_chunks = pl.num_programs(2)
    chunk = jnp.where(direction == 0, step, n_chunks - 1 - step)

    @pl.when(step == 0)
    def _():
        h_sc[...] = jnp.zeros_like(h_sc)

    ext_sc[0:SUBLANES, :] = jnp.where(chunk > 0, prev_ref[0], 0.0)
    ext_sc[SUBLANES:SUBLANES + tc, :] = cur_ref[0]
    ext_sc[SUBLANES + tc:, :] = jnp.where(chunk < n_chunks - 1, next_ref[0], 0.0)
    ext = ext_sc[...]
    xc = jnp.broadcast_to(cb_ref[...], (tc, LRU_WIDTH))
    for tap in range(CONV_WIDTH):
        shifted = pltpu.roll(ext, (CONV_LEFT - tap) % ext.shape[0], 0) if tap != CONV_LEFT else ext
        xc = xc + cw_ref[tap:tap + 1, :] * shifted[SUBLANES:SUBLANES + tc, :]

    z = jnp.dot(xc.astype(BF16), gw_ref[0], preferred_element_type=F32)
    r_gate = _sigmoid(z[:, :LRU_WIDTH] + gb_ref[0, 0:1, :])
    i_gate = _sigmoid(z[:, LRU_WIDTH:] + gb_ref[0, 1:2, :])
    log_a = (-LRU_C) * r_gate * _softplus(-lam_ref[0])
    a = jnp.exp(log_a)
    a_sc[...] = a
    b_sc[...] = jnp.sqrt(-jnp.tanh(log_a) * (a * a + 1.0)) * (i_gate * xc)

    n_groups = tc // SUBLANES

    def run(reverse):
        def body(g, h):
            grp = n_groups - 1 - g if reverse else g
            off = pl.multiple_of(grp * SUBLANES, SUBLANES)
            a, b = _group_scan(a_sc[pl.ds(off, SUBLANES), :], b_sc[pl.ds(off, SUBLANES), :], reverse)
            hs = a * h + b
            o_ref[0, 0, pl.ds(off, SUBLANES), :] = hs
            last = hs[0:1, :] if reverse else hs[SUBLANES - 1:SUBLANES, :]
            return jnp.broadcast_to(last, hs.shape)
        h_sc[...] = lax.fori_loop(0, n_groups, body, h_sc[...], unroll=4)

    @pl.when(direction == 0)
    def _():
        run(False)

    @pl.when(direction == 1)
    def _():
        run(True)


def _lru_scan(xr, conv_w, conv_b, gate_w, gate_b, lam, name):
    b, s, w = xr.shape
    tc = _token_tile(s)
    n_chunks = s // tc
    per_chunk = tc // SUBLANES
    n_rows8 = s // SUBLANES

    def chunk_of(d, i):
        return jnp.where(d == 0, i, n_chunks - 1 - i)

    return pl.pallas_call(
        functools.partial(_lru_kernel, tc=tc),
        grid=(2, b, n_chunks),
        in_specs=[
            pl.BlockSpec((1, tc, w), lambda d, bi, i: (bi, chunk_of(d, i), 0)),
            pl.BlockSpec((1, SUBLANES, w),
                         lambda d, bi, i: (bi, jnp.maximum(chunk_of(d, i) * per_chunk - 1, 0), 0)),
            pl.BlockSpec((1, SUBLANES, w),
                         lambda d, bi, i: (bi, jnp.minimum((chunk_of(d, i) + 1) * per_chunk, n_rows8 - 1), 0)),
            _full((CONV_WIDTH, w)),
            _full((1, w)),
            pl.BlockSpec((1, w, 2 * w), lambda d, bi, i: (d, 0, 0)),
            pl.BlockSpec((1, 2, w), lambda d, bi, i: (d, 0, 0)),
            pl.BlockSpec((1, 1, w), lambda d, bi, i: (d, 0, 0)),
        ],
        out_specs=pl.BlockSpec((1, 1, tc, w), lambda d, bi, i: (d, bi, chunk_of(d, i), 0)),
        out_shape=jax.ShapeDtypeStruct((2, b, s, w), F32),
        scratch_shapes=[pltpu.VMEM((tc + 2 * SUBLANES, w), F32), pltpu.VMEM((tc, w), F32),
                        pltpu.VMEM((tc, w), F32), pltpu.VMEM((SUBLANES, w), F32)],
        compiler_params=_params(("arbitrary", "arbitrary", "arbitrary")),
        name=name,
    )(xr, xr, xr, conv_w, conv_b.reshape(1, w), gate_w, gate_b, lam.reshape(2, 1, w))


def _dense_block_diag(blocks):
    nb, bw, _ = blocks.shape
    eye = jnp.eye(nb, dtype=blocks.dtype)
    return (eye[:, None, :, None] * blocks[:, :, None, :]).reshape(nb * bw, nb * bw)


ROPE_X1_LO = 0
ROPE_X2_LO = HEAD_PAD // 2
NOPE_A = (ROPE_HALF, HEAD_PAD // 2)
NOPE_B = (HEAD_PAD // 2 + ROPE_HALF, QK_DIM)
NOPE_SPLIT = NOPE_A[1] - NOPE_A[0]


def _slot_pad(nope, rope):
    ref = nope if nope is not None else rope
    lead = ref.shape[:-1]
    z = lambda n: jnp.zeros(lead + (n,), ref.dtype)
    nope = z(QK_NOPE_DIM) if nope is None else nope
    rope = z(QK_ROPE_DIM) if rope is None else rope
    return jnp.concatenate([rope[..., :ROPE_HALF], nope[..., :NOPE_SPLIT], rope[..., ROPE_HALF:],
                            nope[..., NOPE_SPLIT:], z(HEAD_PAD - QK_DIM)], axis=-1)


def _rope_table_kernel(pos_ref, freq_ref, cos_t_ref, sin_t_ref, c_ref, s_ref):
    ang = freq_ref[...] * pos_ref[...].astype(F32)
    cos_t, sin_t = jnp.cos(ang), jnp.sin(ang)
    cos_t_ref[...] = cos_t
    sin_t_ref[...] = sin_t
    rest = HEAD_PAD // 2 - ROPE_HALF
    ones = jnp.ones((rest, ang.shape[1]), F32)
    zeros = jnp.zeros((rest, ang.shape[1]), F32)
    c_ref[...] = jnp.concatenate([cos_t, ones, cos_t, ones], axis=0).T
    s_ref[...] = jnp.concatenate([-sin_t, zeros, sin_t, zeros], axis=0).T


def _rope_tables(positions):
    t = positions.size
    tm = _token_tile(t)
    inv_freq = ROPE_THETA ** (-jnp.arange(ROPE_HALF, dtype=F32) * (2.0 / QK_ROPE_DIM))
    col = pl.BlockSpec((ROPE_HALF, tm), lambda i: (0, i))
    row = pl.BlockSpec((tm, HEAD_PAD), lambda i: (i, 0))
    return pl.pallas_call(
        _rope_table_kernel,
        grid=(t // tm,),
        in_specs=[pl.BlockSpec((1, tm), lambda i: (0, i)), _full((ROPE_HALF, 1))],
        out_specs=[col, col, row, row],
        out_shape=[jax.ShapeDtypeStruct((ROPE_HALF, t), F32)] * 2 + [jax.ShapeDtypeStruct((t, HEAD_PAD), F32)] * 2,
        compiler_params=_params(("parallel",)),
        name="rope_tables",
    )(positions.reshape(1, t), inv_freq.reshape(ROPE_HALF, 1))


def _mla_prep_kernel(cq_ref, ckv_ref, kr_ref, cos_t_ref, sin_t_ref, c_ref, s_ref, qa_ref, kva_ref, qg_ref,
                     kg_ref, wq_t_ref, wk_ref, wv_t_ref, qt_ref, k_ref, vt_ref):
    tk = cq_ref.shape[0]
    cqn = _rms(cq_ref[...], qa_ref[...])
    ckn = _rms(ckv_ref[...], kva_ref[...])
    ckn_bf = ckn.astype(BF16)
    yq_t = jnp.dot(wq_t_ref[...], cqn.T.astype(BF16), preferred_element_type=F32)
    yv_t = jnp.dot(wv_t_ref[...], ckn.T.astype(BF16), preferred_element_type=F32)
    yk = jnp.dot(ckn_bf, wk_ref[...], preferred_element_type=F32)
    k_rope = kr_ref[...]
    cos_t, sin_t = cos_t_ref[...], sin_t_ref[...]
    c, s = c_ref[...], s_ref[...]
    q_gain = jnp.broadcast_to(qg_ref[...] * ((QK_DIM ** -0.5) * LOG2E), (HEAD_PAD, tk))
    k_gain = kg_ref[...]
    inv_dim = 1.0 / QK_DIM
    pad_row = lax.broadcasted_iota(jnp.int32, (BF16_SUBLANES, tk), 0)
    ones_rows = jnp.where(pad_row == 0, 1.0, 0.0)
    x1 = slice(ROPE_X1_LO, ROPE_X1_LO + ROPE_HALF)
    x2 = slice(ROPE_X2_LO, ROPE_X2_LO + ROPE_HALF)
    for h in range(MLA_HEADS):
        rows = slice(h * HEAD_PAD, (h + 1) * HEAD_PAD)
        qh = yq_t[rows, :]
        qn = qh * lax.rsqrt(jnp.sum(qh * qh, axis=0, keepdims=True) * inv_dim + EPS) * q_gain
        q1, q2 = qn[x1, :], qn[x2, :]
        qt_ref[0, h] = jnp.concatenate(
            [q1 * cos_t - q2 * sin_t, qn[NOPE_A[0]:NOPE_A[1], :], q2 * cos_t + q1 * sin_t, qn[x2.stop:, :]],
            axis=0).astype(BF16)
        kf = yk[:, rows] + k_rope
        kn = kf * lax.rsqrt(jnp.sum(kf * kf, axis=-1, keepdims=True) * inv_dim + EPS) * k_gain
        k_ref[0, :, rows] = (kn * c + pltpu.roll(kn, HEAD_PAD // 2, 1) * s).astype(BF16)
        vt_ref[0, h, 0] = jnp.concatenate(
            [yv_t[h * V_HEAD_DIM:(h + 1) * V_HEAD_DIM, :], ones_rows], axis=0).astype(BF16)


def _mla_prep(cq, ckv, kr, tables, qa, kva, qg, kg, w_uq, w_ukv, batch, seq, tk, name):
    n_kv = seq // tk
    row = lambda w: pl.BlockSpec((tk, w), lambda b, i: (b * n_kv + i, 0))
    col = pl.BlockSpec((ROPE_HALF, tk), lambda b, i: (0, b * n_kv + i))
    width = MLA_HEADS * HEAD_PAD
    wq = w_uq.reshape(Q_LORA_RANK, MLA_HEADS, QK_DIM)
    wq_t = _slot_pad(wq[..., :QK_NOPE_DIM], wq[..., QK_NOPE_DIM:]).reshape(Q_LORA_RANK, width).T.astype(BF16)
    wkv = w_ukv.reshape(KV_LORA_RANK, MLA_HEADS, QK_NOPE_DIM + V_HEAD_DIM)
    wk = _slot_pad(wkv[..., :QK_NOPE_DIM], None).reshape(KV_LORA_RANK, width).astype(BF16)
    wv_t = wkv[..., QK_NOPE_DIM:].reshape(KV_LORA_RANK, MLA_HEADS * V_HEAD_DIM).T.astype(BF16)
    qg_col = _slot_pad(qg[:QK_NOPE_DIM], qg[QK_NOPE_DIM:]).reshape(HEAD_PAD, 1)
    kg_row = _slot_pad(kg[:QK_NOPE_DIM], kg[QK_NOPE_DIM:]).reshape(1, HEAD_PAD)
    cos_t, sin_t, c, s = tables
    return pl.pallas_call(
        _mla_prep_kernel,
        grid=(batch, n_kv),
        in_specs=[row(Q_LORA_RANK), row(KV_LORA_RANK), row(HEAD_PAD), col, col, row(HEAD_PAD), row(HEAD_PAD),
                  _full((1, Q_LORA_RANK)), _full((1, KV_LORA_RANK)), _full((HEAD_PAD, 1)), _full((1, HEAD_PAD)),
                  _full(wq_t.shape), _full(wk.shape), _full(wv_t.shape)],
        out_specs=[pl.BlockSpec((1, MLA_HEADS, HEAD_PAD, tk), lambda b, i: (b, 0, 0, i)),
                   pl.BlockSpec((1, tk, width), lambda b, i: (b, i, 0)),
                   pl.BlockSpec((1, MLA_HEADS, 1, V_ROWS, tk), lambda b, i: (b, 0, i, 0, 0))],
        out_shape=[jax.ShapeDtypeStruct((batch, MLA_HEADS, HEAD_PAD, seq), BF16),
                   jax.ShapeDtypeStruct((batch, seq, width), BF16),
                   jax.ShapeDtypeStruct((batch, MLA_HEADS, n_kv, V_ROWS, tk), BF16)],
        compiler_params=_params(("parallel", "parallel")),
        name=name,
    )(cq, ckv, kr, cos_t, sin_t, c, s, qa.reshape(1, -1), kva.reshape(1, -1), qg_col, kg_row, wq_t, wk, wv_t)


def _attn_kernel(qt_ref, k_ref, vt_ref, o_ref, s_sc, acc_sc, *, tk, n_kv, pairs_per_trip, blocks_per_trip):
    tq = qt_ref.shape[-1]

    def k_block(j, hh):
        off = pl.multiple_of(j * tk, tk)
        return k_ref[0, pl.ds(off, tk), hh * HEAD_PAD:(hh + 1) * HEAD_PAD]

    def write_out():
        out_t = jnp.concatenate(
            [acc_sc[hh, :V_HEAD_DIM, :] / acc_sc[hh, V_HEAD_DIM:V_HEAD_DIM + 1, :] for hh in range(2)], axis=0)
        o_ref[0] = out_t.T.astype(o_ref.dtype)

    def qk(j, hh):
        return jnp.dot(k_block(j, hh), qt_ref[0, hh], preferred_element_type=F32)

    def finish(j, hh, s_t, m_ref, excess):
        col_max = jnp.max(s_t, axis=0, keepdims=True)
        p = jnp.exp2(s_t - m_ref)
        pv = jnp.dot(vt_ref[0, hh, j], p.astype(BF16), preferred_element_type=F32)
        m_new = jnp.maximum(m_ref, col_max)
        acc_sc[hh] = (acc_sc[hh] + pv) * jnp.exp2(m_ref - m_new)
        return m_new, jnp.maximum(excess, col_max - m_ref)

    def lagged_blocks(first, count, m_ref, excess):
        m_ref, excess = list(m_ref), list(excess)
        pending = None
        for u in range(count):
            for hh in range(2):
                s_t = qk(first + u, hh)
                if pending is not None:
                    pj, ph, ps = pending
                    m_ref[ph], excess[ph] = finish(pj, ph, ps, m_ref[ph], excess[ph])
                pending = (first + u, hh, s_t)
        pj, ph, ps = pending
        m_ref[ph], excess[ph] = finish(pj, ph, ps, m_ref[ph], excess[ph])
        return tuple(m_ref), tuple(excess)

    m_ref = tuple(
        jnp.max(jnp.dot(k_ref[0, 0:BF16_SUBLANES, hh * HEAD_PAD:(hh + 1) * HEAD_PAD], qt_ref[0, hh],
                        preferred_element_type=F32), axis=0, keepdims=True) for hh in range(2))
    acc_sc[...] = jnp.zeros_like(acc_sc)
    lag_state = (m_ref, tuple(jnp.zeros((1, tq), F32) for _ in range(2)))

    def lagged_trip(it, carry):
        return lagged_blocks(it * blocks_per_trip, blocks_per_trip, *carry)

    n_lag_trips = n_kv // blocks_per_trip
    lag_state = lax.fori_loop(0, n_lag_trips, lagged_trip, lag_state)
    n_left = n_kv - n_lag_trips * blocks_per_trip
    if n_left:
        lag_state = lagged_blocks(n_kv - n_left, n_left, *lag_state)
    write_out()
    worst_excess = jnp.max(jnp.maximum(lag_state[1][0], lag_state[1][1]))

    def scores(j, slot):
        col_max = []
        for hh in range(2):
            s_t = jnp.dot(k_block(j, hh), qt_ref[0, hh], preferred_element_type=F32)
            s_sc[slot, hh] = s_t
            col_max.append(jnp.max(s_t, axis=0, keepdims=True))
        return tuple(col_max)

    def consume(j, slot, col_max, m_run):
        new = []
        for hh in range(2):
            m_new = jnp.maximum(m_run[hh], col_max[hh])
            alpha = jnp.exp2(m_run[hh] - m_new)
            p = jnp.exp2(s_sc[slot, hh] - m_new)
            pv = jnp.dot(vt_ref[0, hh, j], p.astype(BF16), preferred_element_type=F32)
            acc_sc[hh] = alpha * acc_sc[hh] + pv
            new.append(m_new)
        return tuple(new)

    def pair(j, max_even, state):
        max_odd = scores(j + 1, 1)
        state = consume(j, 0, max_even, state)
        max_even = scores(j + 2, 0)
        state = consume(j + 1, 1, max_odd, state)
        return max_even, state

    def pair_trip(it, carry):
        for u in range(pairs_per_trip):
            carry = pair(2 * (it * pairs_per_trip + u), *carry)
        return carry

    @pl.when(worst_excess > LAG_LIMIT)
    def _():
        acc_sc[...] = jnp.zeros_like(acc_sc)
        state = tuple(jnp.full((1, tq), -jnp.inf, F32) for _ in range(2))
        max_even = scores(0, 0)
        n_full_pairs = n_kv // 2 - 1
        n_trips = n_full_pairs // pairs_per_trip
        max_even, state = lax.fori_loop(0, n_trips, pair_trip, (max_even, state))
        for p in range(n_trips * pairs_per_trip, n_full_pairs):
            max_even, state = pair(2 * p, max_even, state)
        max_odd = scores(n_kv - 1, 1)
        state = consume(n_kv - 2, 0, max_even, state)
        consume(n_kv - 1, 1, max_odd, state)
        write_out()


def _attention(q_t, k, v_t, tq, name):
    batch, heads, _, seq = q_t.shape
    n_kv, tk = v_t.shape[2], v_t.shape[4]
    assert n_kv % 2 == 0, "the kv loop is unrolled in pairs of blocks"
    pairs = heads // 2
    return pl.pallas_call(
        functools.partial(_attn_kernel, tk=tk, n_kv=n_kv, pairs_per_trip=ATTN_PAIRS_PER_TRIP,
                          blocks_per_trip=ATTN_BLOCKS_PER_TRIP),
        grid=(batch, pairs, seq // tq),
        in_specs=[pl.BlockSpec((1, 2, HEAD_PAD, tq), lambda b, hp, qi: (b, hp, 0, qi)),
                  pl.BlockSpec((1, seq, 2 * HEAD_PAD), lambda b, hp, qi: (b, 0, hp)),
                  pl.BlockSpec((1, 2, n_kv, V_ROWS, tk), lambda b, hp, qi: (b, hp, 0, 0, 0))],
        out_specs=pl.BlockSpec((1, tq, 2 * V_HEAD_DIM), lambda b, hp, qi: (b, qi, hp)),
        out_shape=jax.ShapeDtypeStruct((batch, seq, heads * V_HEAD_DIM), BF16),
        scratch_shapes=[pltpu.VMEM((2, 2, tk, tq), F32), pltpu.VMEM((2, V_ROWS, tq), F32)],
        compiler_params=_params(("parallel", "parallel", "arbitrary")),
        name=name,
    )(q_t, k, v_t)


def _mla_in_weight(w_in):
    c_q = w_in[:, :Q_LORA_RANK]
    c_kv = w_in[:, Q_LORA_RANK:Q_LORA_RANK + KV_LORA_RANK]
    k_rope = w_in[:, Q_LORA_RANK + KV_LORA_RANK:Q_LORA_RANK + KV_LORA_RANK + QK_ROPE_DIM]
    u_mem = w_in[:, Q_LORA_RANK + KV_LORA_RANK + QK_ROPE_DIM:]
    return jnp.concatenate([c_q, c_kv, u_mem, _slot_pad(None, k_rope)], axis=1).astype(BF16)


def kernel(x, mem, positions, ffn1_norm, ffn1_w_gate_up, ffn1_w_down, mix_norm, mem_norm, w_mem_kv,
           mem_q_norm, mem_k_norm, w_out, ffn2_norm, ffn2_w_gate_up, ffn2_w_down, lru_w_in, lru_conv_w,
           lru_conv_b, lru_gate_w, lru_gate_b, lru_lambda, mla_w_in, mla_q_a_norm, mla_w_uq,
           mla_kv_a_norm, mla_w_ukv, mla_q_norm, mla_k_norm):
    batch, seq, d = x.shape
    depth = ffn1_norm.shape[0]
    t = batch * seq
    xf = x.reshape(t, d)
    tables = _rope_tables(positions)
    tk = _token_tile(seq)

    for layer in range(depth):
        j = layer // 2
        xf = _ffn(xf, ffn1_norm[layer], ffn1_w_gate_up[layer].astype(BF16),
                  ffn1_w_down[layer].astype(BF16), f"ffn1_{layer}")
        kt, v = _mem_kv(mem, mem_norm[layer], w_mem_kv[layer].astype(BF16), mem_k_norm[layer],
                        f"mem_kv_{layer}")
        if layer % 2 == 0:
            gate, xr, u_mem = _rms_proj(xf, mix_norm[layer], lru_w_in[j].astype(BF16),
                                        (LRU_WIDTH, LRU_WIDTH, MEM_WIDTH), f"lru_in_{layer}")
            gw = jnp.stack([
                jnp.concatenate([_dense_block_diag(lru_gate_w[j, dr, 0]),
                                 _dense_block_diag(lru_gate_w[j, dr, 1])], axis=1)
                for dr in range(2)]).astype(BF16)
            h_dirs = _lru_scan(xr.reshape(batch, seq, LRU_WIDTH), lru_conv_w[j], lru_conv_b[j], gw,
                               lru_gate_b[j], lru_lambda[j], f"lru_scan_{layer}")
            toks = (h_dirs.reshape(2, t, LRU_WIDTH), gate)
        else:
            cq, ckv, u_mem, kr = _rms_proj(xf, mix_norm[layer], _mla_in_weight(mla_w_in[j]),
                                           (Q_LORA_RANK, KV_LORA_RANK, MEM_WIDTH, HEAD_PAD),
                                           f"mla_in_{layer}")
            q_t, k, v_t = _mla_prep(cq, ckv, kr, tables, mla_q_a_norm[j], mla_kv_a_norm[j],
                                    mla_q_norm[j], mla_k_norm[j], mla_w_uq[j], mla_w_ukv[j],
                                    batch, seq, tk, f"mla_prep_{layer}")
            attn = _attention(q_t, k, v_t, tk, f"mla_attn_{layer}")
            toks = (attn.reshape(t, TOK_WIDTH),)
        xf = _mix_out(xf, toks, u_mem, mem_q_norm[layer], kt, v, w_out[layer].astype(BF16), seq,
                      f"mix_out_{layer}")
        xf = _ffn(xf, ffn2_norm[layer], ffn2_w_gate_up[layer].astype(BF16),
                  ffn2_w_down[layer].astype(BF16), f"ffn2_{layer}")
    return xf.reshape(batch, seq, d)
```

```python
import functools
import math

import jax
import jax.numpy as jnp
from jax import lax
from jax.experimental import pallas as pl
from jax.experimental.pallas import tpu as pltpu

F32 = jnp.float32
BF16 = jnp.bfloat16

D_MODEL = 1024
MEM_LEN = 256
MEM_HEADS = 4
MEM_HEAD_DIM = 64
MEM_WIDTH = MEM_HEADS * MEM_HEAD_DIM
TOK_WIDTH = D_MODEL - MEM_WIDTH
LRU_WIDTH = TOK_WIDTH
LRU_BLOCKS = 8
LRU_BLOCK_DIM = LRU_WIDTH // LRU_BLOCKS
CONV_WIDTH = 4
CONV_LEFT = 2
LRU_C = 8.0
MLA_HEADS = 12
QK_NOPE_DIM = 64
QK_ROPE_DIM = 32
QK_DIM = QK_NOPE_DIM + QK_ROPE_DIM
V_HEAD_DIM = TOK_WIDTH // MLA_HEADS
Q_LORA_RANK = 384
KV_LORA_RANK = 256
ROPE_THETA = 10000.0
D_FF = 2816
EPS = 1e-6

LANES = 128
SUBLANES = 8
HEAD_PAD = LANES
BF16_SUBLANES = 2 * SUBLANES
V_ROWS = V_HEAD_DIM + BF16_SUBLANES
ROPE_HALF = QK_ROPE_DIM // 2
VMEM_LIMIT = 56 * 1024 * 1024
LOG2E = math.log2(math.e)
ATTN_BLOCKS_PER_TRIP = 16
LAG_LIMIT = 64.0
ATTN_PAIRS_PER_TRIP = 1


def _params(sem, vmem=VMEM_LIMIT):
    return pltpu.CompilerParams(dimension_semantics=sem, vmem_limit_bytes=vmem)


def _rms(x, g):
    ms = jnp.mean(x * x, axis=-1, keepdims=True)
    return x * lax.rsqrt(ms + EPS) * g


def _full(shape):
    zeros = (0,) * len(shape)
    return pl.BlockSpec(shape, lambda *_: zeros)


def _token_tile(t):
    for tm in (512, 256, 128, 64, 32, 16, 8):
        if t % tm == 0:
            return tm
    raise ValueError(f"token count {t} must be a multiple of 8")


def _rms_proj_kernel(x_ref, g_ref, w_ref, *o_refs, splits):
    h = _rms(x_ref[...], g_ref[...]).astype(BF16)
    y = jnp.dot(h, w_ref[...], preferred_element_type=F32)
    off = 0
    for o_ref, n in zip(o_refs, splits):
        o_ref[...] = y[:, off:off + n].astype(o_ref.dtype)
        off += n


def _rms_proj(x, g, w, splits, name):
    t, d = x.shape
    n = w.shape[1]
    assert sum(splits) == n and all(s % LANES == 0 for s in splits)
    tm = _token_tile(t)
    return pl.pallas_call(
        functools.partial(_rms_proj_kernel, splits=splits),
        grid=(t // tm,),
        in_specs=[pl.BlockSpec((tm, d), lambda i: (i, 0)), _full((1, d)), _full((d, n))],
        out_specs=[pl.BlockSpec((tm, s), lambda i: (i, 0)) for s in splits],
        out_shape=[jax.ShapeDtypeStruct((t, s), F32) for s in splits],
        compiler_params=_params(("parallel",)),
        name=name,
    )(x, g.reshape(1, d), w)


def _ffn_kernel(x_ref, g_ref, wgu_ref, wd_ref, o_ref):
    x = x_ref[...]
    h = _rms(x, g_ref[...]).astype(BF16)
    gu = jnp.dot(h, wgu_ref[...], preferred_element_type=F32)
    gate, up = gu[:, :D_FF], gu[:, D_FF:]
    act = (gate * jax.nn.sigmoid(gate) * up).astype(BF16)
    y = jnp.dot(act, wd_ref[...], preferred_element_type=F32)
    o_ref[...] = x + 0.5 * y


def _ffn(x, g, wgu, wd, name):
    t, d = x.shape
    tm = _token_tile(t)
    once = pl.Buffered(1)
    return pl.pallas_call(
        _ffn_kernel,
        grid=(t // tm,),
        in_specs=[
            pl.BlockSpec((tm, d), lambda i: (i, 0)),
            _full((1, d)),
            pl.BlockSpec(wgu.shape, lambda i: (0, 0), pipeline_mode=once),
            pl.BlockSpec(wd.shape, lambda i: (0, 0), pipeline_mode=once),
        ],
        out_specs=pl.BlockSpec((tm, d), lambda i: (i, 0)),
        out_shape=jax.ShapeDtypeStruct((t, d), F32),
        compiler_params=_params(("parallel",)),
        name=name,
    )(x, g.reshape(1, d), wgu, wd)


def _head_rms_scale(x, head_dim):
    lane = lax.broadcasted_iota(jnp.int32, x.shape, 1)
    x2 = x * x
    scale = jnp.zeros_like(x)
    for h in range(x.shape[1] // head_dim):
        in_head = (lane >= h * head_dim) & (lane < (h + 1) * head_dim)
        ss = jnp.sum(jnp.where(in_head, x2, 0.0), axis=-1, keepdims=True)
        scale = jnp.where(in_head, lax.rsqrt(ss * (1.0 / head_dim) + EPS), scale)
    return scale


def _mem_kv_kernel(mem_ref, g_ref, w_ref, kg_ref, kt_ref, v_ref):
    mn = _rms(mem_ref[0], g_ref[...]).astype(BF16)
    kv = jnp.dot(mn, w_ref[...], preferred_element_type=F32)
    k, v = kv[:, :MEM_WIDTH], kv[:, MEM_WIDTH:]
    kn = k * _head_rms_scale(k, MEM_HEAD_DIM) * kg_ref[...]
    kt_ref[0] = kn.T.astype(BF16)
    v_ref[0] = v.astype(BF16)


def _mem_kv(mem, g, w, k_gain, name):
    b, m, d = mem.shape
    return pl.pallas_call(
        _mem_kv_kernel,
        grid=(b,),
        in_specs=[pl.BlockSpec((1, m, d), lambda i: (i, 0, 0)), _full((1, d)),
                  _full(w.shape), _full((1, MEM_WIDTH))],
        out_specs=[pl.BlockSpec((1, MEM_WIDTH, m), lambda i: (i, 0, 0)),
                   pl.BlockSpec((1, m, MEM_WIDTH), lambda i: (i, 0, 0))],
        out_shape=[jax.ShapeDtypeStruct((b, MEM_WIDTH, m), BF16),
                   jax.ShapeDtypeStruct((b, m, MEM_WIDTH), BF16)],
        compiler_params=_params(("parallel",)),
        name=name,
    )(mem, g.reshape(1, d), w, jnp.tile(k_gain, MEM_HEADS).reshape(1, MEM_WIDTH))


def _gelu_tanh(x):
    return 0.5 * x * (1.0 + jnp.tanh(math.sqrt(2.0 / math.pi) * (x + 0.044715 * (x * x * x))))


def _mem_attention(q, qg, kt, v):
    lane = lax.broadcasted_iota(jnp.int32, q.shape, 1)
    qn = q * _head_rms_scale(q, MEM_HEAD_DIM) * (qg * (MEM_HEAD_DIM ** -0.5))
    out = jnp.zeros_like(q)
    for h in range(MEM_HEADS):
        in_head = (lane >= h * MEM_HEAD_DIM) & (lane < (h + 1) * MEM_HEAD_DIM)
        qh = jnp.where(in_head, qn, 0.0).astype(BF16)
        s = jnp.dot(qh, kt, preferred_element_type=F32)
        p = jnp.exp(s - jnp.max(s, axis=-1, keepdims=True))
        denom = jnp.sum(p, axis=-1, keepdims=True)
        oh = jnp.dot(p.astype(BF16), v, preferred_element_type=F32)
        out = jnp.where(in_head, oh / denom, out)
    return out


def _mix_out_lru_kernel(x_ref, hf_ref, hr_ref, gate_ref, um_ref, qg_ref, kt_ref, v_ref,
                        wt_ref, wm_ref, o_ref):
    tok = (hf_ref[0] + hr_ref[0]) * _gelu_tanh(gate_ref[...])
    mem_out = _mem_attention(um_ref[...], qg_ref[...], kt_ref[0], v_ref[0])
    y = jnp.dot(tok.astype(BF16), wt_ref[...], preferred_element_type=F32)
    y = y + jnp.dot(mem_out.astype(BF16), wm_ref[...], preferred_element_type=F32)
    o_ref[...] = x_ref[...] + y


def _mix_out_mla_kernel(x_ref, tok_ref, um_ref, qg_ref, kt_ref, v_ref, wt_ref, wm_ref, o_ref):
    mem_out = _mem_attention(um_ref[...], qg_ref[...], kt_ref[0], v_ref[0])
    y = jnp.dot(tok_ref[...], wt_ref[...], preferred_element_type=F32)
    y = y + jnp.dot(mem_out.astype(BF16), wm_ref[...], preferred_element_type=F32)
    o_ref[...] = x_ref[...] + y


def _mix_out(x, toks, u_mem, q_gain, kt, v, w_out, seq, name):
    t, d = x.shape
    tm = _token_tile(seq)
    per_batch = seq // tm
    row = lambda w: pl.BlockSpec((tm, w), lambda i: (i, 0))
    per_b = lambda shape: pl.BlockSpec((1,) + shape, lambda i: (i // per_batch, 0, 0))
    common_specs = [row(MEM_WIDTH), _full((1, MEM_WIDTH)), per_b(kt.shape[1:]), per_b(v.shape[1:]),
                    _full((TOK_WIDTH, d)), _full((MEM_WIDTH, d))]
    common_args = (u_mem, jnp.tile(q_gain, MEM_HEADS).reshape(1, MEM_WIDTH), kt, v,
                   w_out[:TOK_WIDTH], w_out[TOK_WIDTH:])
    if len(toks) == 2:
        h_dirs, gate = toks
        kern = _mix_out_lru_kernel
        tok_specs = [pl.BlockSpec((1, tm, TOK_WIDTH), lambda i: (0, i, 0)),
                     pl.BlockSpec((1, tm, TOK_WIDTH), lambda i: (1, i, 0)), row(TOK_WIDTH)]
        tok_args = (h_dirs, h_dirs, gate)
    else:
        kern = _mix_out_mla_kernel
        tok_specs = [row(TOK_WIDTH)]
        tok_args = toks
    return pl.pallas_call(
        kern,
        grid=(t // tm,),
        in_specs=[row(d)] + tok_specs + common_specs,
        out_specs=row(d),
        out_shape=jax.ShapeDtypeStruct((t, d), F32),
        compiler_params=_params(("parallel",)),
        name=name,
    )(x, *tok_args, *common_args)


def _sigmoid(x):
    return 0.5 * jnp.tanh(0.5 * x) + 0.5


def _softplus(x):
    return jnp.maximum(x, 0.0) + jnp.log1p(jnp.exp(-jnp.abs(x)))


def _group_scan(a, b, reverse):
    row = lax.broadcasted_iota(jnp.int32, a.shape, 0)
    for dist in (1, 2, 4):
        if reverse:
            valid = row < SUBLANES - dist
            shift = SUBLANES - dist
        else:
            valid = row >= dist
            shift = dist
        a_sh = jnp.where(valid, pltpu.roll(a, shift, 0), 1.0)
        b_sh = jnp.where(valid, pltpu.roll(b, shift, 0), 0.0)
        b = a * b_sh + b
        a = a * a_sh
    return a, b


def _lru_kernel(cur_ref, prev_ref, next_ref, cw_ref, cb_ref, gw_ref, gb_ref, lam_ref, o_ref,
                ext_sc, a_sc, b_sc, h_sc, *, tc):
    direction = pl.program_id(0)
    step = pl.program_id(2)
    n_chunks = pl.num_programs(2)
    chunk = jnp.where(direction == 0, step, n_chunks - 1 - step)

    @pl.when(step == 0)
    def _():
        h_sc[...] = jnp.zeros_like(h_sc)

    ext_sc[0:SUBLANES, :] = jnp.where(chunk > 0, prev_ref[0], 0.0)
    ext_sc[SUBLANES:SUBLANES + tc, :] = cur_ref[0]
    ext_sc[SUBLANES + tc:, :] = jnp.where(chunk < n_chunks - 1, next_ref[0], 0.0)
    ext = ext_sc[...]
    xc = jnp.broadcast_to(cb_ref[...], (tc, LRU_WIDTH))
    for tap in range(CONV_WIDTH):
        shifted = pltpu.roll(ext, (CONV_LEFT - tap) % ext.shape[0], 0) if tap != CONV_LEFT else ext
        xc = xc + cw_ref[tap:tap + 1, :] * shifted[SUBLANES:SUBLANES + tc, :]

    z = jnp.dot(xc.astype(BF16), gw_ref[0], preferred_element_type=F32)
    r_gate = _sigmoid(z[:, :LRU_WIDTH] + gb_ref[0, 0:1, :])
    i_gate = _sigmoid(z[:, LRU_WIDTH:] + gb_ref[0, 1:2, :])
    log_a = (-LRU_C) * r_gate * _softplus(-lam_ref[0])
    a = jnp.exp(log_a)
    a_sc[...] = a
    b_sc[...] = jnp.sqrt(-jnp.tanh(log_a) * (a * a + 1.0)) * (i_gate * xc)

    n_groups = tc // SUBLANES

    def run(reverse):
        def body(g, h):
            grp = n_groups - 1 - g if reverse else g
            off = pl.multiple_of(grp * SUBLANES, SUBLANES)
            a, b = _group_scan(a_sc[pl.ds(off, SUBLANES), :], b_sc[pl.ds(off, SUBLANES), :], reverse)
            hs = a * h + b
            o_ref[0, 0, pl.ds(off, SUBLANES), :] = hs
            last = hs[0:1, :] if reverse else hs[SUBLANES - 1:SUBLANES, :]
            return jnp.broadcast_to(last, hs.shape)
        h_sc[...] = lax.fori_loop(0, n_groups, body, h_sc[...], unroll=4)

    @pl.when(direction == 0)
    def _():
        run(False)

    @pl.when(direction == 1)
    def _():
        run(True)


def _lru_scan(xr, conv_w, conv_b, gate_w, gate_b, lam, name):
    b, s, w = xr.shape
    tc = _token_tile(s)
    n_chunks = s // tc
    per_chunk = tc // SUBLANES
    n_rows8 = s // SUBLANES

    def chunk_of(d, i):
        return jnp.where(d == 0, i, n_chunks - 1 - i)

    return pl.pallas_call(
        functools.partial(_lru_kernel, tc=tc),
        grid=(2, b, n_chunks),
        in_specs=[
            pl.BlockSpec((1, tc, w), lambda d, bi, i: (bi, chunk_of(d, i), 0)),
            pl.BlockSpec((1, SUBLANES, w),
                         lambda d, bi, i: (bi, jnp.maximum(chunk_of(d, i) * per_chunk - 1, 0), 0)),
            pl.BlockSpec((1, SUBLANES, w),
                         lambda d, bi, i: (bi, jnp.minimum((chunk_of(d, i) + 1) * per_chunk, n_rows8 - 1), 0)),
            _full((CONV_WIDTH, w)),
            _full((1, w)),
            pl.BlockSpec((1, w, 2 * w), lambda d, bi, i: (d, 0, 0)),
            pl.BlockSpec((1, 2, w), lambda d, bi, i: (d, 0, 0)),
            pl.BlockSpec((1, 1, w), lambda d, bi, i: (d, 0, 0)),
        ],
        out_specs=pl.BlockSpec((1, 1, tc, w), lambda d, bi, i: (d, bi, chunk_of(d, i), 0)),
        out_shape=jax.ShapeDtypeStruct((2, b, s, w), F32),
        scratch_shapes=[pltpu.VMEM((tc + 2 * SUBLANES, w), F32), pltpu.VMEM((tc, w), F32),
                        pltpu.VMEM((tc, w), F32), pltpu.VMEM((SUBLANES, w), F32)],
        compiler_params=_params(("arbitrary", "arbitrary", "arbitrary")),
        name=name,
    )(xr, xr, xr, conv_w, conv_b.reshape(1, w), gate_w, gate_b, lam.reshape(2, 1, w))


def _dense_block_diag(blocks):
    nb, bw, _ = blocks.shape
    eye = jnp.eye(nb, dtype=blocks.dtype)
    return (eye[:, None, :, None] * blocks[:, :, None, :]).reshape(nb * bw, nb * bw)


ROPE_X1_LO = 0
ROPE_X2_LO = HEAD_PAD // 2
NOPE_A = (ROPE_HALF, HEAD_PAD // 2)
NOPE_B = (HEAD_PAD // 2 + ROPE_HALF, QK_DIM)
NOPE_SPLIT = NOPE_A[1] - NOPE_A[0]


def _slot_pad(nope, rope):
    ref = nope if nope is not None else rope
    lead = ref.shape[:-1]
    z = lambda n: jnp.zeros(lead + (n,), ref.dtype)
    nope = z(QK_NOPE_DIM) if nope is None else nope
    rope = z(QK_ROPE_DIM) if rope is None else rope
    return jnp.concatenate([rope[..., :ROPE_HALF], nope[..., :NOPE_SPLIT], rope[..., ROPE_HALF:],
                            nope[..., NOPE_SPLIT:], z(HEAD_PAD - QK_DIM)], axis=-1)


def _rope_table_kernel(pos_ref, freq_ref, cos_t_ref, sin_t_ref):
    ang = freq_ref[...] * pos_ref[...].astype(F32)
    cos_t_ref[...] = jnp.cos(ang)
    sin_t_ref[...] = jnp.sin(ang)


def _rope_tables(positions):
    t = positions.size
    tm = _token_tile(t)
    inv_freq = ROPE_THETA ** (-jnp.arange(ROPE_HALF, dtype=F32) * (2.0 / QK_ROPE_DIM))
    col = pl.BlockSpec((ROPE_HALF, tm), lambda i: (0, i))
    return pl.pallas_call(
        _rope_table_kernel,
        grid=(t // tm,),
        in_specs=[pl.BlockSpec((1, tm), lambda i: (0, i)), _full((ROPE_HALF, 1))],
        out_specs=[col, col],
        out_shape=[jax.ShapeDtypeStruct((ROPE_HALF, t), F32)] * 2,
        compiler_params=_params(("parallel",)),
        name="rope_tables",
    )(positions.reshape(1, t), inv_freq.reshape(ROPE_HALF, 1))


def _head_norm_rope_t(y, gain, cos_t, sin_t):
    yn = y * lax.rsqrt(jnp.sum(y * y, axis=0, keepdims=True) * (1.0 / QK_DIM) + EPS) * gain
    y1 = yn[ROPE_X1_LO:ROPE_X1_LO + ROPE_HALF, :]
    y2 = yn[ROPE_X2_LO:ROPE_X2_LO + ROPE_HALF, :]
    return jnp.concatenate([y1 * cos_t - y2 * sin_t, yn[NOPE_A[0]:NOPE_A[1], :],
                            y2 * cos_t + y1 * sin_t, yn[ROPE_X2_LO + ROPE_HALF:, :]], axis=0)


def _mla_prep_kernel(cq_ref, ckv_ref, kr_ref, cos_t_ref, sin_t_ref, qa_ref, kva_ref, qg_ref, kg_ref,
                     wq_t_ref, wk_t_ref, wv_t_ref, qt_ref, k_ref, vt_ref):
    tk = cq_ref.shape[0]
    cqn_t = _rms(cq_ref[...], qa_ref[...]).T.astype(BF16)
    ckn_t = _rms(ckv_ref[...], kva_ref[...]).T.astype(BF16)
    yq_t = jnp.dot(wq_t_ref[...], cqn_t, preferred_element_type=F32)
    yk_t = jnp.dot(wk_t_ref[...], ckn_t, preferred_element_type=F32)
    yv_t = jnp.dot(wv_t_ref[...], ckn_t, preferred_element_type=F32)
    k_rope_t = kr_ref[...].T
    cos_t, sin_t = cos_t_ref[...], sin_t_ref[...]
    q_gain = jnp.broadcast_to(qg_ref[...] * ((QK_DIM ** -0.5) * LOG2E), (HEAD_PAD, tk))
    k_gain = jnp.broadcast_to(kg_ref[...], (HEAD_PAD, tk))
    pad_row = lax.broadcasted_iota(jnp.int32, (BF16_SUBLANES, tk), 0)
    ones_rows = jnp.where(pad_row == 0, 1.0, 0.0)
    for h in range(MLA_HEADS):
        rows = slice(h * HEAD_PAD, (h + 1) * HEAD_PAD)
        qt_ref[0, h] = _head_norm_rope_t(yq_t[rows, :], q_gain, cos_t, sin_t).astype(BF16)
        k_t = _head_norm_rope_t(yk_t[rows, :] + k_rope_t, k_gain, cos_t, sin_t)
        k_ref[0, :, rows] = k_t.T.astype(BF16)
        vt_ref[0, h, 0] = jnp.concatenate(
            [yv_t[h * V_HEAD_DIM:(h + 1) * V_HEAD_DIM, :], ones_rows], axis=0).astype(BF16)


def _mla_prep(cq, ckv, kr, tables, qa, kva, qg, kg, w_uq, w_ukv, batch, seq, tk, name):
    n_kv = seq // tk
    row = lambda w: pl.BlockSpec((tk, w), lambda b, i: (b * n_kv + i, 0))
    col = pl.BlockSpec((ROPE_HALF, tk), lambda b, i: (0, b * n_kv + i))
    width = MLA_HEADS * HEAD_PAD
    wq = w_uq.reshape(Q_LORA_RANK, MLA_HEADS, QK_DIM)
    wq_t = _slot_pad(wq[..., :QK_NOPE_DIM], wq[..., QK_NOPE_DIM:]).reshape(Q_LORA_RANK, width).T.astype(BF16)
    wkv = w_ukv.reshape(KV_LORA_RANK, MLA_HEADS, QK_NOPE_DIM + V_HEAD_DIM)
    wk_t = _slot_pad(wkv[..., :QK_NOPE_DIM], None).reshape(KV_LORA_RANK, width).T.astype(BF16)
    wv_t = wkv[..., QK_NOPE_DIM:].reshape(KV_LORA_RANK, MLA_HEADS * V_HEAD_DIM).T.astype(BF16)
    qg_col = _slot_pad(qg[:QK_NOPE_DIM], qg[QK_NOPE_DIM:]).reshape(HEAD_PAD, 1)
    kg_col = _slot_pad(kg[:QK_NOPE_DIM], kg[QK_NOPE_DIM:]).reshape(HEAD_PAD, 1)
    cos_t, sin_t = tables
    return pl.pallas_call(
        _mla_prep_kernel,
        grid=(batch, n_kv),
        in_specs=[row(Q_LORA_RANK), row(KV_LORA_RANK), row(HEAD_PAD), col, col,
                  _full((1, Q_LORA_RANK)), _full((1, KV_LORA_RANK)), _full((HEAD_PAD, 1)), _full((HEAD_PAD, 1)),
                  _full(wq_t.shape), _full(wk_t.shape), _full(wv_t.shape)],
        out_specs=[pl.BlockSpec((1, MLA_HEADS, HEAD_PAD, tk), lambda b, i: (b, 0, 0, i)),
                   pl.BlockSpec((1, tk, width), lambda b, i: (b, i, 0)),
                   pl.BlockSpec((1, MLA_HEADS, 1, V_ROWS, tk), lambda b, i: (b, 0, i, 0, 0))],
        out_shape=[jax.ShapeDtypeStruct((batch, MLA_HEADS, HEAD_PAD, seq), BF16),
                   jax.ShapeDtypeStruct((batch, seq, width), BF16),
                   jax.ShapeDtypeStruct((batch, MLA_HEADS, n_kv, V_ROWS, tk), BF16)],
        compiler_params=_params(("parallel", "parallel")),
        name=name,
    )(cq, ckv, kr, cos_t, sin_t, qa.reshape(1, -1), kva.reshape(1, -1), qg_col, kg_col, wq_t, wk_t, wv_t)


def _attn_kernel(qt_ref, k_ref, vt_ref, o_ref, s_sc, acc_sc, *, tk, n_kv, pairs_per_trip, blocks_per_trip):
    tq = qt_ref.shape[-1]

    def k_block(j, hh):
        off = pl.multiple_of(j * tk, tk)
        return k_ref[0, pl.ds(off, tk), hh * HEAD_PAD:(hh + 1) * HEAD_PAD]

    def write_out():
        out_t = jnp.concatenate(
            [acc_sc[hh, :V_HEAD_DIM, :] / acc_sc[hh, V_HEAD_DIM:V_HEAD_DIM + 1, :] for hh in range(2)], axis=0)
        o_ref[0] = out_t.T.astype(o_ref.dtype)

    def qk(j, hh):
        return jnp.dot(k_block(j, hh), qt_ref[0, hh], preferred_element_type=F32)

    def finish(j, hh, s_t, m_ref, excess):
        p = jnp.exp2(s_t - m_ref).astype(BF16)
        pv = jnp.dot(vt_ref[0, hh, j], p, preferred_element_type=F32)
        rise = jnp.maximum(jnp.log2(jnp.max(p, axis=0, keepdims=True).astype(F32)), 0.0)
        acc_sc[hh] = (acc_sc[hh] + pv) * jnp.exp2(-rise)
        return m_ref + rise, jnp.maximum(excess, rise)

    def lagged_blocks(first, count, m_ref, excess):
        m_ref, excess = list(m_ref), list(excess)
        pending = None
        for u in range(count):
            for hh in range(2):
                s_t = qk(first + u, hh)
                if pending is not None:
                    pj, ph, ps = pending
                    m_ref[ph], excess[ph] = finish(pj, ph, ps, m_ref[ph], excess[ph])
                pending = (first + u, hh, s_t)
        pj, ph, ps = pending
        m_ref[ph], excess[ph] = finish(pj, ph, ps, m_ref[ph], excess[ph])
        return tuple(m_ref), tuple(excess)

    m_ref = tuple(
        jnp.max(jnp.dot(k_ref[0, 0:BF16_SUBLANES, hh * HEAD_PAD:(hh + 1) * HEAD_PAD], qt_ref[0, hh],
                        preferred_element_type=F32), axis=0, keepdims=True) for hh in range(2))
    acc_sc[...] = jnp.zeros_like(acc_sc)
    lag_state = (m_ref, tuple(jnp.zeros((1, tq), F32) for _ in range(2)))

    def lagged_trip(it, carry):
        return lagged_blocks(it * blocks_per_trip, blocks_per_trip, *carry)

    n_lag_trips = n_kv // blocks_per_trip
    lag_state = lax.fori_loop(0, n_lag_trips, lagged_trip, lag_state)
    n_left = n_kv - n_lag_trips * blocks_per_trip
    if n_left:
        lag_state = lagged_blocks(n_kv - n_left, n_left, *lag_state)
    write_out()
    worst_excess = jnp.max(jnp.maximum(lag_state[1][0], lag_state[1][1]))

    def scores(j, slot):
        col_max = []
        for hh in range(2):
            s_t = jnp.dot(k_block(j, hh), qt_ref[0, hh], preferred_element_type=F32)
            s_sc[slot, hh] = s_t
            col_max.append(jnp.max(s_t, axis=0, keepdims=True))
        return tuple(col_max)

    def consume(j, slot, col_max, m_run):
        new = []
        for hh in range(2):
            m_new = jnp.maximum(m_run[hh], col_max[hh])
            alpha = jnp.exp2(m_run[hh] - m_new)
            p = jnp.exp2(s_sc[slot, hh] - m_new)
            pv = jnp.dot(vt_ref[0, hh, j], p.astype(BF16), preferred_element_type=F32)
            acc_sc[hh] = alpha * acc_sc[hh] + pv
            new.append(m_new)
        return tuple(new)

    def pair(j, max_even, state):
        max_odd = scores(j + 1, 1)
        state = consume(j, 0, max_even, state)
        max_even = scores(j + 2, 0)
        state = consume(j + 1, 1, max_odd, state)
        return max_even, state

    def pair_trip(it, carry):
        for u in range(pairs_per_trip):
            carry = pair(2 * (it * pairs_per_trip + u), *carry)
        return carry

    @pl.when(worst_excess > LAG_LIMIT)
    def _():
        acc_sc[...] = jnp.zeros_like(acc_sc)
        state = tuple(jnp.full((1, tq), -jnp.inf, F32) for _ in range(2))
        max_even = scores(0, 0)
        n_full_pairs = n_kv // 2 - 1
        n_trips = n_full_pairs // pairs_per_trip
        max_even, state = lax.fori_loop(0, n_trips, pair_trip, (max_even, state))
        for p in range(n_trips * pairs_per_trip, n_full_pairs):
            max_even, state = pair(2 * p, max_even, state)
        max_odd = scores(n_kv - 1, 1)
        state = consume(n_kv - 2, 0, max_even, state)
        consume(n_kv - 1, 1, max_odd, state)
        write_out()


def _attention(q_t, k, v_t, tq, name):
    batch, heads, _, seq = q_t.shape
    n_kv, tk = v_t.shape[2], v_t.shape[4]
    assert n_kv % 2 == 0, "the kv loop is unrolled in pairs of blocks"
    pairs = heads // 2
    return pl.pallas_call(
        functools.partial(_attn_kernel, tk=tk, n_kv=n_kv, pairs_per_trip=ATTN_PAIRS_PER_TRIP,
                          blocks_per_trip=ATTN_BLOCKS_PER_TRIP),
        grid=(batch, pairs, seq // tq),
        in_specs=[pl.BlockSpec((1, 2, HEAD_PAD, tq), lambda b, hp, qi: (b, hp, 0, qi)),
                  pl.BlockSpec((1, seq, 2 * HEAD_PAD), lambda b, hp, qi: (b, 0, hp)),
                  pl.BlockSpec((1, 2, n_kv, V_ROWS, tk), lambda b, hp, qi: (b, hp, 0, 0, 0))],
        out_specs=pl.BlockSpec((1, tq, 2 * V_HEAD_DIM), lambda b, hp, qi: (b, qi, hp)),
        out_shape=jax.ShapeDtypeStruct((batch, seq, heads * V_HEAD_DIM), BF16),
        scratch_shapes=[pltpu.VMEM((2, 2, tk, tq), F32), pltpu.VMEM((2, V_ROWS, tq), F32)],
        compiler_params=_params(("parallel", "parallel", "arbitrary")),
        name=name,
    )(q_t, k, v_t)


def _mla_in_weight(w_in):
    c_q = w_in[:, :Q_LORA_RANK]
    c_kv = w_in[:, Q_LORA_RANK:Q_LORA_RANK + KV_LORA_RANK]
    k_rope = w_in[:, Q_LORA_RANK + KV_LORA_RANK:Q_LORA_RANK + KV_LORA_RANK + QK_ROPE_DIM]
    u_mem = w_in[:, Q_LORA_RANK + KV_LORA_RANK + QK_ROPE_DIM:]
    return jnp.concatenate([c_q, c_kv, u_mem, _slot_pad(None, k_rope)], axis=1).astype(BF16)


def kernel(x, mem, positions, ffn1_norm, ffn1_w_gate_up, ffn1_w_down, mix_norm, mem_norm, w_mem_kv,
           mem_q_norm, mem_k_norm, w_out, ffn2_norm, ffn2_w_gate_up, ffn2_w_down, lru_w_in, lru_conv_w,
           lru_conv_b, lru_gate_w, lru_gate_b, lru_lambda, mla_w_in, mla_q_a_norm, mla_w_uq,
           mla_kv_a_norm, mla_w_ukv, mla_q_norm, mla_k_norm):
    batch, seq, d = x.shape
    depth = ffn1_norm.shape[0]
    t = batch * seq
    xf = x.reshape(t, d)
    tables = _rope_tables(positions)
    tk = _token_tile(seq)

    for layer in range(depth):
        j = layer // 2
        xf = _ffn(xf, ffn1_norm[layer], ffn1_w_gate_up[layer].astype(BF16),
                  ffn1_w_down[layer].astype(BF16), f"ffn1_{layer}")
        kt, v = _mem_kv(mem, mem_norm[layer], w_mem_kv[layer].astype(BF16), mem_k_norm[layer],
                        f"mem_kv_{layer}")
        if layer % 2 == 0:
            gate, xr, u_mem = _rms_proj(xf, mix_norm[layer], lru_w_in[j].astype(BF16),
                                        (LRU_WIDTH, LRU_WIDTH, MEM_WIDTH), f"lru_in_{layer}")
            gw = jnp.stack([
                jnp.concatenate([_dense_block_diag(lru_gate_w[j, dr, 0]),
                                 _dense_block_diag(lru_gate_w[j, dr, 1])], axis=1)
                for dr in range(2)]).astype(BF16)
            h_dirs = _lru_scan(xr.reshape(batch, seq, LRU_WIDTH), lru_conv_w[j], lru_conv_b[j], gw,
                               lru_gate_b[j], lru_lambda[j], f"lru_scan_{layer}")
            toks = (h_dirs.reshape(2, t, LRU_WIDTH), gate)
        else:
            cq, ckv, u_mem, kr = _rms_proj(xf, mix_norm[layer], _mla_in_weight(mla_w_in[j]),
                                           (Q_LORA_RANK, KV_LORA_RANK, MEM_WIDTH, HEAD_PAD),
                                           f"mla_in_{layer}")
            q_t, k, v_t = _mla_prep(cq, ckv, kr, tables, mla_q_a_norm[j], mla_kv_a_norm[j],
                                    mla_q_norm[j], mla_k_norm[j], mla_w_uq[j], mla_w_ukv[j],
                                    batch, seq, tk, f"mla_prep_{layer}")
            attn = _attention(q_t, k, v_t, tk, f"mla_attn_{layer}")
            toks = (attn.reshape(t, TOK_WIDTH),)
        xf = _mix_out(xf, toks, u_mem, mem_q_norm[layer], kt, v, w_out[layer].astype(BF16), seq,
                      f"mix_out_{layer}")
        xf = _ffn(xf, ffn2_norm[layer], ffn2_w_gate_up[layer].astype(BF16),
                  ffn2_w_down[layer].astype(BF16), f"ffn2_{layer}")
    return xf.reshape(batch, seq, d)
```

```python
import functools
import math

import jax
import jax.numpy as jnp
from jax import lax
from jax.experimental import pallas as pl
from jax.experimental.pallas import tpu as pltpu

F32 = jnp.float32
BF16 = jnp.bfloat16

D_MODEL = 1024
MEM_LEN = 256
MEM_HEADS = 4
MEM_HEAD_DIM = 64
MEM_WIDTH = MEM_HEADS * MEM_HEAD_DIM
TOK_WIDTH = D_MODEL - MEM_WIDTH
LRU_WIDTH = TOK_WIDTH
LRU_BLOCKS = 8
LRU_BLOCK_DIM = LRU_WIDTH // LRU_BLOCKS
CONV_WIDTH = 4
CONV_LEFT = 2
LRU_C = 8.0
MLA_HEADS = 12
QK_NOPE_DIM = 64
QK_ROPE_DIM = 32
QK_DIM = QK_NOPE_DIM + QK_ROPE_DIM
V_HEAD_DIM = TOK_WIDTH // MLA_HEADS
Q_LORA_RANK = 384
KV_LORA_RANK = 256
ROPE_THETA = 10000.0
D_FF = 2816
EPS = 1e-6

LANES = 128
SUBLANES = 8
HEAD_PAD = LANES
BF16_SUBLANES = 2 * SUBLANES
V_ROWS = V_HEAD_DIM + BF16_SUBLANES
ROPE_HALF = QK_ROPE_DIM // 2
VMEM_LIMIT = 56 * 1024 * 1024
LOG2E = math.log2(math.e)
ATTN_BLOCKS_PER_TRIP = 32
LAG_LIMIT = 64.0
ATTN_PAIRS_PER_TRIP = 1


def _params(sem, vmem=VMEM_LIMIT):
    return pltpu.CompilerParams(dimension_semantics=sem, vmem_limit_bytes=vmem)


def _rms(x, g):
    ms = jnp.mean(x * x, axis=-1, keepdims=True)
    return x * lax.rsqrt(ms + EPS) * g


def _full(shape):
    zeros = (0,) * len(shape)
    return pl.BlockSpec(shape, lambda *_: zeros)


def _token_tile(t):
    for tm in (512, 256, 128, 64, 32, 16, 8):
        if t % tm == 0:
            return tm
    raise ValueError(f"token count {t} must be a multiple of 8")


def _ffn_kernel(x_ref, g_ref, wgu_ref, wd_ref, *rest, splits):
    x = x_ref[...]
    h = _rms(x, g_ref[...]).astype(BF16)
    gu = jnp.dot(h, wgu_ref[...], preferred_element_type=F32)
    gate, up = gu[:, :D_FF], gu[:, D_FF:]
    act = (gate * jax.nn.sigmoid(gate) * up).astype(BF16)
    x_new = x + 0.5 * jnp.dot(act, wd_ref[...], preferred_element_type=F32)
    if not splits:
        (o_ref,) = rest
        o_ref[...] = x_new
        return
    g_mix_ref, w_in_ref, o_ref, *u_refs = rest
    o_ref[...] = x_new
    u = jnp.dot(_rms(x_new, g_mix_ref[...]).astype(BF16), w_in_ref[...], preferred_element_type=F32)
    off = 0
    for u_ref, n in zip(u_refs, splits):
        u_ref[...] = u[:, off:off + n]
        off += n


def _ffn(x, g, wgu, wd, name, g_mix=None, w_in=None, splits=()):
    t, d = x.shape
    tm = _token_tile(t)
    once = pl.Buffered(1)
    row = lambda n: pl.BlockSpec((tm, n), lambda i: (i, 0))
    resident = lambda w: pl.BlockSpec(w.shape, lambda i: (0, 0), pipeline_mode=once)
    in_specs = [row(d), _full((1, d)), resident(wgu), resident(wd)]
    args = [x, g.reshape(1, d), wgu, wd]
    if splits:
        assert sum(splits) == w_in.shape[1] and all(s % LANES == 0 for s in splits)
        in_specs += [_full((1, d)), resident(w_in)]
        args += [g_mix.reshape(1, d), w_in]
    outs = pl.pallas_call(
        functools.partial(_ffn_kernel, splits=tuple(splits)),
        grid=(t // tm,),
        in_specs=in_specs,
        out_specs=[row(d)] + [row(s) for s in splits],
        out_shape=[jax.ShapeDtypeStruct((t, d), F32)] + [jax.ShapeDtypeStruct((t, s), F32) for s in splits],
        compiler_params=_params(("parallel",)),
        name=name,
    )(*args)
    return outs if splits else outs[0]


def _head_rms_scale(x, head_dim):
    lane = lax.broadcasted_iota(jnp.int32, x.shape, 1)
    x2 = x * x
    scale = jnp.zeros_like(x)
    for h in range(x.shape[1] // head_dim):
        in_head = (lane >= h * head_dim) & (lane < (h + 1) * head_dim)
        ss = jnp.sum(jnp.where(in_head, x2, 0.0), axis=-1, keepdims=True)
        scale = jnp.where(in_head, lax.rsqrt(ss * (1.0 / head_dim) + EPS), scale)
    return scale


def _mem_kv_kernel(mem_ref, g_ref, w_ref, kg_ref, kt_ref, v_ref):
    mn = _rms(mem_ref[0], g_ref[...]).astype(BF16)
    kv = jnp.dot(mn, w_ref[...], preferred_element_type=F32)
    k, v = kv[:, :MEM_WIDTH], kv[:, MEM_WIDTH:]
    kn = k * _head_rms_scale(k, MEM_HEAD_DIM) * kg_ref[...]
    kt_ref[0] = kn.T.astype(BF16)
    v_ref[0] = v.astype(BF16)


def _mem_kv(mem, g, w, k_gain, name):
    b, m, d = mem.shape
    return pl.pallas_call(
        _mem_kv_kernel,
        grid=(b,),
        in_specs=[pl.BlockSpec((1, m, d), lambda i: (i, 0, 0)), _full((1, d)),
                  _full(w.shape), _full((1, MEM_WIDTH))],
        out_specs=[pl.BlockSpec((1, MEM_WIDTH, m), lambda i: (i, 0, 0)),
                   pl.BlockSpec((1, m, MEM_WIDTH), lambda i: (i, 0, 0))],
        out_shape=[jax.ShapeDtypeStruct((b, MEM_WIDTH, m), BF16),
                   jax.ShapeDtypeStruct((b, m, MEM_WIDTH), BF16)],
        compiler_params=_params(("parallel",)),
        name=name,
    )(mem, g.reshape(1, d), w, jnp.tile(k_gain, MEM_HEADS).reshape(1, MEM_WIDTH))


def _gelu_tanh(x):
    return 0.5 * x * (1.0 + jnp.tanh(math.sqrt(2.0 / math.pi) * (x + 0.044715 * (x * x * x))))


def _mem_attention(q, qg, kt, v):
    lane = lax.broadcasted_iota(jnp.int32, q.shape, 1)
    qn = q * _head_rms_scale(q, MEM_HEAD_DIM) * (qg * (MEM_HEAD_DIM ** -0.5))
    out = jnp.zeros_like(q)
    for h in range(MEM_HEADS):
        in_head = (lane >= h * MEM_HEAD_DIM) & (lane < (h + 1) * MEM_HEAD_DIM)
        qh = jnp.where(in_head, qn, 0.0).astype(BF16)
        s = jnp.dot(qh, kt, preferred_element_type=F32)
        p = jnp.exp(s - jnp.max(s, axis=-1, keepdims=True))
        denom = jnp.sum(p, axis=-1, keepdims=True)
        oh = jnp.dot(p.astype(BF16), v, preferred_element_type=F32)
        out = jnp.where(in_head, oh / denom, out)
    return out


def _mix_out_lru_kernel(x_ref, hf_ref, hr_ref, gate_ref, um_ref, qg_ref, kt_ref, v_ref,
                        wt_ref, wm_ref, o_ref):
    tok = (hf_ref[0] + hr_ref[0]) * _gelu_tanh(gate_ref[...])
    mem_out = _mem_attention(um_ref[...], qg_ref[...], kt_ref[0], v_ref[0])
    y = jnp.dot(tok.astype(BF16), wt_ref[...], preferred_element_type=F32)
    y = y + jnp.dot(mem_out.astype(BF16), wm_ref[...], preferred_element_type=F32)
    o_ref[...] = x_ref[...] + y


def _mix_out_mla_kernel(x_ref, tok_ref, um_ref, qg_ref, kt_ref, v_ref, wt_ref, wm_ref, o_ref):
    mem_out = _mem_attention(um_ref[...], qg_ref[...], kt_ref[0], v_ref[0])
    y = jnp.dot(tok_ref[...], wt_ref[...], preferred_element_type=F32)
    y = y + jnp.dot(mem_out.astype(BF16), wm_ref[...], preferred_element_type=F32)
    o_ref[...] = x_ref[...] + y


def _mix_out(x, toks, u_mem, q_gain, kt, v, w_out, seq, name):
    t, d = x.shape
    tm = _token_tile(seq)
    per_batch = seq // tm
    row = lambda w: pl.BlockSpec((tm, w), lambda i: (i, 0))
    per_b = lambda shape: pl.BlockSpec((1,) + shape, lambda i: (i // per_batch, 0, 0))
    common_specs = [row(MEM_WIDTH), _full((1, MEM_WIDTH)), per_b(kt.shape[1:]), per_b(v.shape[1:]),
                    _full((TOK_WIDTH, d)), _full((MEM_WIDTH, d))]
    common_args = (u_mem, jnp.tile(q_gain, MEM_HEADS).reshape(1, MEM_WIDTH), kt, v,
                   w_out[:TOK_WIDTH], w_out[TOK_WIDTH:])
    if len(toks) == 2:
        h_dirs, gate = toks
        kern = _mix_out_lru_kernel
        tok_specs = [pl.BlockSpec((1, tm, TOK_WIDTH), lambda i: (0, i, 0)),
                     pl.BlockSpec((1, tm, TOK_WIDTH), lambda i: (1, i, 0)), row(TOK_WIDTH)]
        tok_args = (h_dirs, h_dirs, gate)
    else:
        kern = _mix_out_mla_kernel
        tok_specs = [row(TOK_WIDTH)]
        tok_args = toks
    return pl.pallas_call(
        kern,
        grid=(t // tm,),
        in_specs=[row(d)] + tok_specs + common_specs,
        out_specs=row(d),
        out_shape=jax.ShapeDtypeStruct((t, d), F32),
        compiler_params=_params(("parallel",)),
        name=name,
    )(x, *tok_args, *common_args)


def _sigmoid(x):
    return 0.5 * jnp.tanh(0.5 * x) + 0.5


def _softplus(x):
    return jnp.maximum(x, 0.0) + jnp.log1p(jnp.exp(-jnp.abs(x)))


def _group_scan(a, b, reverse):
    row = lax.broadcasted_iota(jnp.int32, a.shape, 0)
    for dist in (1, 2, 4):
        if reverse:
            valid = row < SUBLANES - dist
            shift = SUBLANES - dist
        else:
            valid = row >= dist
            shift = dist
        a_sh = jnp.where(valid, pltpu.roll(a, shift, 0), 1.0)
        b_sh = jnp.where(valid, pltpu.roll(b, shift, 0), 0.0)
        b = a * b_sh + b
        a = a * a_sh
    return a, b


def _lru_kernel(cur_ref, prev_ref, next_ref, cw_ref, cb_ref, gw_ref, gb_ref, lam_ref, o_ref,
                ext_sc, a_sc, b_sc, h_sc, *, tc):
    direction = pl.program_id(0)
    step = pl.program_id(2)
    n_chunks = pl.num_programs(2)
    chunk = jnp.where(direction == 0, step, n_chunks - 1 - step)

    @pl.when(step == 0)
    def _():
        h_sc[...] = jnp.zeros_like(h_sc)

    ext_sc[0:SUBLANES, :] = jnp.where(chunk > 0, prev_ref[0], 0.0)
    ext_sc[SUBLANES:SUBLANES + tc, :] = cur_ref[0]
    ext_sc[SUBLANES + tc:, :] = jnp.where(chunk < n_chunks - 1, next_ref[0], 0.0)
    ext = ext_sc[...]
    xc = jnp.broadcast_to(cb_ref[...], (tc, LRU_WIDTH))
    for tap in range(CONV_WIDTH):
        shifted = pltpu.roll(ext, (CONV_LEFT - tap) % ext.shape[0], 0) if tap != CONV_LEFT else ext
        xc = xc + cw_ref[tap:tap + 1, :] * shifted[SUBLANES:SUBLANES + tc, :]

    z = jnp.dot(xc.astype(BF16), gw_ref[0], preferred_element_type=F32)
    r_gate = _sigmoid(z[:, :LRU_WIDTH] + gb_ref[0, 0:1, :])
    i_gate = _sigmoid(z[:, LRU_WIDTH:] + gb_ref[0, 1:2, :])
    log_a = (-LRU_C) * r_gate * _softplus(-lam_ref[0])
    a = jnp.exp(log_a)
    a_sc[...] = a
    b_sc[...] = jnp.sqrt(-jnp.tanh(log_a) * (a * a + 1.0)) * (i_gate * xc)

    n_groups = tc // SUBLANES

    def run(reverse):
        def body(g, h):
            grp = n_groups - 1 - g if reverse else g
            off = pl.multiple_of(grp * SUBLANES, SUBLANES)
            a, b = _group_scan(a_sc[pl.ds(off, SUBLANES), :], b_sc[pl.ds(off, SUBLANES), :], reverse)
            hs = a * h + b
            o_ref[0, 0, pl.ds(off, SUBLANES), :] = hs
            last = hs[0:1, :] if reverse else hs[SUBLANES - 1:SUBLANES, :]
            return jnp.broadcast_to(last, hs.shape)
        h_sc[...] = lax.fori_loop(0, n_groups, body, h_sc[...], unroll=4)

    @pl.when(direction == 0)
    def _():
        run(False)

    @pl.when(direction == 1)
    def _():
        run(True)


def _lru_scan(xr, conv_w, conv_b, gate_w, gate_b, lam, name):
    b, s, w = xr.shape
    tc = _token_tile(s)
    n_chunks = s // tc
    per_chunk = tc // SUBLANES
    n_rows8 = s // SUBLANES

    def chunk_of(d, i):
        return jnp.where(d == 0, i, n_chunks - 1 - i)

    return pl.pallas_call(
        functools.partial(_lru_kernel, tc=tc),
        grid=(2, b, n_chunks),
        in_specs=[
            pl.BlockSpec((1, tc, w), lambda d, bi, i: (bi, chunk_of(d, i), 0)),
            pl.BlockSpec((1, SUBLANES, w),
                         lambda d, bi, i: (bi, jnp.maximum(chunk_of(d, i) * per_chunk - 1, 0), 0)),
            pl.BlockSpec((1, SUBLANES, w),
                         lambda d, bi, i: (bi, jnp.minimum((chunk_of(d, i) + 1) * per_chunk, n_rows8 - 1), 0)),
            _full((CONV_WIDTH, w)),
            _full((1, w)),
            pl.BlockSpec((1, w, 2 * w), lambda d, bi, i: (d, 0, 0)),
            pl.BlockSpec((1, 2, w), lambda d, bi, i: (d, 0, 0)),
            pl.BlockSpec((1, 1, w), lambda d, bi, i: (d, 0, 0)),
        ],
        out_specs=pl.BlockSpec((1, 1, tc, w), lambda d, bi, i: (d, bi, chunk_of(d, i), 0)),
        out_shape=jax.ShapeDtypeStruct((2, b, s, w), F32),
        scratch_shapes=[pltpu.VMEM((tc + 2 * SUBLANES, w), F32), pltpu.VMEM((tc, w), F32),
                        pltpu.VMEM((tc, w), F32), pltpu.VMEM((SUBLANES, w), F32)],
        compiler_params=_params(("arbitrary", "arbitrary", "arbitrary")),
        name=name,
    )(xr, xr, xr, conv_w, conv_b.reshape(1, w), gate_w, gate_b, lam.reshape(2, 1, w))


def _dense_block_diag(blocks):
    nb, bw, _ = blocks.shape
    eye = jnp.eye(nb, dtype=blocks.dtype)
    return (eye[:, None, :, None] * blocks[:, :, None, :]).reshape(nb * bw, nb * bw)


ROPE_X1_LO = 0
ROPE_X2_LO = HEAD_PAD // 2
NOPE_A = (ROPE_HALF, HEAD_PAD // 2)
NOPE_B = (HEAD_PAD // 2 + ROPE_HALF, QK_DIM)
NOPE_SPLIT = NOPE_A[1] - NOPE_A[0]


def _slot_pad(nope, rope):
    ref = nope if nope is not None else rope
    lead = ref.shape[:-1]
    z = lambda n: jnp.zeros(lead + (n,), ref.dtype)
    nope = z(QK_NOPE_DIM) if nope is None else nope
    rope = z(QK_ROPE_DIM) if rope is None else rope
    return jnp.concatenate([rope[..., :ROPE_HALF], nope[..., :NOPE_SPLIT], rope[..., ROPE_HALF:],
                            nope[..., NOPE_SPLIT:], z(HEAD_PAD - QK_DIM)], axis=-1)


def _rope_table_kernel(pos_ref, freq_ref, cos_t_ref, sin_t_ref):
    ang = freq_ref[...] * pos_ref[...].astype(F32)
    cos_t_ref[...] = jnp.cos(ang)
    sin_t_ref[...] = jnp.sin(ang)


def _rope_tables(positions):
    t = positions.size
    tm = _token_tile(t)
    inv_freq = ROPE_THETA ** (-jnp.arange(ROPE_HALF, dtype=F32) * (2.0 / QK_ROPE_DIM))
    col = pl.BlockSpec((ROPE_HALF, tm), lambda i: (0, i))
    return pl.pallas_call(
        _rope_table_kernel,
        grid=(t // tm,),
        in_specs=[pl.BlockSpec((1, tm), lambda i: (0, i)), _full((ROPE_HALF, 1))],
        out_specs=[col, col],
        out_shape=[jax.ShapeDtypeStruct((ROPE_HALF, t), F32)] * 2,
        compiler_params=_params(("parallel",)),
        name="rope_tables",
    )(positions.reshape(1, t), inv_freq.reshape(ROPE_HALF, 1))


def _head_norm_rope_t(y, gain, cos_t, sin_t):
    yn = y * lax.rsqrt(jnp.sum(y * y, axis=0, keepdims=True) * (1.0 / QK_DIM) + EPS) * gain
    y1 = yn[ROPE_X1_LO:ROPE_X1_LO + ROPE_HALF, :]
    y2 = yn[ROPE_X2_LO:ROPE_X2_LO + ROPE_HALF, :]
    return jnp.concatenate([y1 * cos_t - y2 * sin_t, yn[NOPE_A[0]:NOPE_A[1], :],
                            y2 * cos_t + y1 * sin_t, yn[ROPE_X2_LO + ROPE_HALF:, :]], axis=0)


def _mla_prep_kernel(cq_ref, ckv_ref, kr_ref, cos_t_ref, sin_t_ref, qa_ref, kva_ref, qg_ref, kg_ref,
                     wq_t_ref, wk_t_ref, wv_t_ref, qt_ref, k_ref, vt_ref):
    tk = cq_ref.shape[0]
    cqn_t = _rms(cq_ref[...], qa_ref[...]).T.astype(BF16)
    ckn_t = _rms(ckv_ref[...], kva_ref[...]).T.astype(BF16)
    yq_t = jnp.dot(wq_t_ref[...], cqn_t, preferred_element_type=F32)
    yk_t = jnp.dot(wk_t_ref[...], ckn_t, preferred_element_type=F32)
    yv_t = jnp.dot(wv_t_ref[...], ckn_t, preferred_element_type=F32)
    k_rope_t = kr_ref[...].T
    cos_t, sin_t = cos_t_ref[...], sin_t_ref[...]
    q_gain = jnp.broadcast_to(qg_ref[...] * ((QK_DIM ** -0.5) * LOG2E), (HEAD_PAD, tk))
    k_gain = jnp.broadcast_to(kg_ref[...], (HEAD_PAD, tk))
    pad_row = lax.broadcasted_iota(jnp.int32, (BF16_SUBLANES, tk), 0)
    ones_rows = jnp.where(pad_row == 0, 1.0, 0.0)
    for h in range(MLA_HEADS):
        rows = slice(h * HEAD_PAD, (h + 1) * HEAD_PAD)
        qt_ref[0, h] = _head_norm_rope_t(yq_t[rows, :], q_gain, cos_t, sin_t).astype(BF16)
        k_t = _head_norm_rope_t(yk_t[rows, :] + k_rope_t, k_gain, cos_t, sin_t)
        k_ref[0, :, rows] = k_t.T.astype(BF16)
        vt_ref[0, h, 0] = jnp.concatenate(
            [yv_t[h * V_HEAD_DIM:(h + 1) * V_HEAD_DIM, :], ones_rows], axis=0).astype(BF16)


def _mla_prep(cq, ckv, kr, tables, qa, kva, qg, kg, w_uq, w_ukv, batch, seq, tk, name):
    n_kv = seq // tk
    row = lambda w: pl.BlockSpec((tk, w), lambda b, i: (b * n_kv + i, 0))
    col = pl.BlockSpec((ROPE_HALF, tk), lambda b, i: (0, b * n_kv + i))
    width = MLA_HEADS * HEAD_PAD
    wq = w_uq.reshape(Q_LORA_RANK, MLA_HEADS, QK_DIM)
    wq_t = _slot_pad(wq[..., :QK_NOPE_DIM], wq[..., QK_NOPE_DIM:]).reshape(Q_LORA_RANK, width).T.astype(BF16)
    wkv = w_ukv.reshape(KV_LORA_RANK, MLA_HEADS, QK_NOPE_DIM + V_HEAD_DIM)
    wk_t = _slot_pad(wkv[..., :QK_NOPE_DIM], None).reshape(KV_LORA_RANK, width).T.astype(BF16)
    wv_t = wkv[..., QK_NOPE_DIM:].reshape(KV_LORA_RANK, MLA_HEADS * V_HEAD_DIM).T.astype(BF16)
    qg_col = _slot_pad(qg[:QK_NOPE_DIM], qg[QK_NOPE_DIM:]).reshape(HEAD_PAD, 1)
    kg_col = _slot_pad(kg[:QK_NOPE_DIM], kg[QK_NOPE_DIM:]).reshape(HEAD_PAD, 1)
    cos_t, sin_t = tables
    return pl.pallas_call(
        _mla_prep_kernel,
        grid=(batch, n_kv),
        in_specs=[row(Q_LORA_RANK), row(KV_LORA_RANK), row(HEAD_PAD), col, col,
                  _full((1, Q_LORA_RANK)), _full((1, KV_LORA_RANK)), _full((HEAD_PAD, 1)), _full((HEAD_PAD, 1)),
                  _full(wq_t.shape), _full(wk_t.shape), _full(wv_t.shape)],
        out_specs=[pl.BlockSpec((1, MLA_HEADS, HEAD_PAD, tk), lambda b, i: (b, 0, 0, i)),
                   pl.BlockSpec((1, tk, width), lambda b, i: (b, i, 0)),
                   pl.BlockSpec((1, MLA_HEADS, 1, V_ROWS, tk), lambda b, i: (b, 0, i, 0, 0))],
        out_shape=[jax.ShapeDtypeStruct((batch, MLA_HEADS, HEAD_PAD, seq), BF16),
                   jax.ShapeDtypeStruct((batch, seq, width), BF16),
                   jax.ShapeDtypeStruct((batch, MLA_HEADS, n_kv, V_ROWS, tk), BF16)],
        compiler_params=_params(("parallel", "parallel")),
        name=name,
    )(cq, ckv, kr, cos_t, sin_t, qa.reshape(1, -1), kva.reshape(1, -1), qg_col, kg_col, wq_t, wk_t, wv_t)


def _attn_kernel(qt_ref, k_ref, vt_ref, o_ref, s_sc, acc_sc, *, tk, n_kv, pairs_per_trip, blocks_per_trip):
    tq = qt_ref.shape[-1]

    def k_block(j, hh):
        off = pl.multiple_of(j * tk, tk)
        return k_ref[0, pl.ds(off, tk), hh * HEAD_PAD:(hh + 1) * HEAD_PAD]

    def write_out():
        out_t = jnp.concatenate(
            [acc_sc[hh, :V_HEAD_DIM, :] / acc_sc[hh, V_HEAD_DIM:V_HEAD_DIM + 1, :] for hh in range(2)], axis=0)
        o_ref[0] = out_t.T.astype(o_ref.dtype)

    def qk(j, hh):
        return jnp.dot(k_block(j, hh), qt_ref[0, hh], preferred_element_type=F32)

    def finish(j, hh, s_t, m_ref, excess):
        p = jnp.exp2(s_t - m_ref).astype(BF16)
        pv = jnp.dot(vt_ref[0, hh, j], p, preferred_element_type=F32)
        rise = jnp.maximum(jnp.log2(jnp.max(p, axis=0, keepdims=True).astype(F32)), 0.0)
        acc_sc[hh] = (acc_sc[hh] + pv) * jnp.exp2(-rise)
        return m_ref + rise, jnp.maximum(excess, rise)

    def lagged_blocks(first, count, m_ref, excess):
        m_ref, excess = list(m_ref), list(excess)
        pending = None
        for u in range(count):
            for hh in range(2):
                s_t = qk(first + u, hh)
                if pending is not None:
                    pj, ph, ps = pending
                    m_ref[ph], excess[ph] = finish(pj, ph, ps, m_ref[ph], excess[ph])
                pending = (first + u, hh, s_t)
        pj, ph, ps = pending
        m_ref[ph], excess[ph] = finish(pj, ph, ps, m_ref[ph], excess[ph])
        return tuple(m_ref), tuple(excess)

    m_ref = tuple(
        jnp.max(jnp.dot(k_ref[0, 0:BF16_SUBLANES, hh * HEAD_PAD:(hh + 1) * HEAD_PAD], qt_ref[0, hh],
                        preferred_element_type=F32), axis=0, keepdims=True) for hh in range(2))
    acc_sc[...] = jnp.zeros_like(acc_sc)
    lag_state = (m_ref, tuple(jnp.zeros((1, tq), F32) for _ in range(2)))

    def lagged_trip(it, carry):
        return lagged_blocks(it * blocks_per_trip, blocks_per_trip, *carry)

    n_lag_trips = n_kv // blocks_per_trip
    lag_state = lax.fori_loop(0, n_lag_trips, lagged_trip, lag_state)
    n_left = n_kv - n_lag_trips * blocks_per_trip
    if n_left:
        lag_state = lagged_blocks(n_kv - n_left, n_left, *lag_state)
    write_out()
    worst_excess = jnp.max(jnp.maximum(lag_state[1][0], lag_state[1][1]))

    def scores(j, slot):
        col_max = []
        for hh in range(2):
            s_t = jnp.dot(k_block(j, hh), qt_ref[0, hh], preferred_element_type=F32)
            s_sc[slot, hh] = s_t
            col_max.append(jnp.max(s_t, axis=0, keepdims=True))
        return tuple(col_max)

    def consume(j, slot, col_max, m_run):
        new = []
        for hh in range(2):
            m_new = jnp.maximum(m_run[hh], col_max[hh])
            alpha = jnp.exp2(m_run[hh] - m_new)
            p = jnp.exp2(s_sc[slot, hh] - m_new)
            pv = jnp.dot(vt_ref[0, hh, j], p.astype(BF16), preferred_element_type=F32)
            acc_sc[hh] = alpha * acc_sc[hh] + pv
            new.append(m_new)
        return tuple(new)

    def pair(j, max_even, state):
        max_odd = scores(j + 1, 1)
        state = consume(j, 0, max_even, state)
        max_even = scores(j + 2, 0)
        state = consume(j + 1, 1, max_odd, state)
        return max_even, state

    def pair_trip(it, carry):
        for u in range(pairs_per_trip):
            carry = pair(2 * (it * pairs_per_trip + u), *carry)
        return carry

    @pl.when(worst_excess > LAG_LIMIT)
    def _():
        acc_sc[...] = jnp.zeros_like(acc_sc)
        state = tuple(jnp.full((1, tq), -jnp.inf, F32) for _ in range(2))
        max_even = scores(0, 0)
        n_full_pairs = n_kv // 2 - 1
        n_trips = n_full_pairs // pairs_per_trip
        max_even, state = lax.fori_loop(0, n_trips, pair_trip, (max_even, state))
        for p in range(n_trips * pairs_per_trip, n_full_pairs):
            max_even, state = pair(2 * p, max_even, state)
        max_odd = scores(n_kv - 1, 1)
        state = consume(n_kv - 2, 0, max_even, state)
        consume(n_kv - 1, 1, max_odd, state)
        write_out()


def _attention(q_t, k, v_t, tq, name):
    batch, heads, _, seq = q_t.shape
    n_kv, tk = v_t.shape[2], v_t.shape[4]
    assert n_kv % 2 == 0, "the kv loop is unrolled in pairs of blocks"
    pairs = heads // 2
    return pl.pallas_call(
        functools.partial(_attn_kernel, tk=tk, n_kv=n_kv, pairs_per_trip=ATTN_PAIRS_PER_TRIP,
                          blocks_per_trip=ATTN_BLOCKS_PER_TRIP),
        grid=(batch, pairs, seq // tq),
        in_specs=[pl.BlockSpec((1, 2, HEAD_PAD, tq), lambda b, hp, qi: (b, hp, 0, qi)),
                  pl.BlockSpec((1, seq, 2 * HEAD_PAD), lambda b, hp, qi: (b, 0, hp)),
                  pl.BlockSpec((1, 2, n_kv, V_ROWS, tk), lambda b, hp, qi: (b, hp, 0, 0, 0))],
        out_specs=pl.BlockSpec((1, tq, 2 * V_HEAD_DIM), lambda b, hp, qi: (b, qi, hp)),
        out_shape=jax.ShapeDtypeStruct((batch, seq, heads * V_HEAD_DIM), BF16),
        scratch_shapes=[pltpu.VMEM((2, 2, tk, tq), F32), pltpu.VMEM((2, V_ROWS, tq), F32)],
        compiler_params=_params(("parallel", "parallel", "arbitrary")),
        name=name,
    )(q_t, k, v_t)


def _mla_in_weight(w_in):
    c_q = w_in[:, :Q_LORA_RANK]
    c_kv = w_in[:, Q_LORA_RANK:Q_LORA_RANK + KV_LORA_RANK]
    k_rope = w_in[:, Q_LORA_RANK + KV_LORA_RANK:Q_LORA_RANK + KV_LORA_RANK + QK_ROPE_DIM]
    u_mem = w_in[:, Q_LORA_RANK + KV_LORA_RANK + QK_ROPE_DIM:]
    return jnp.concatenate([c_q, c_kv, u_mem, _slot_pad(None, k_rope)], axis=1).astype(BF16)


def kernel(x, mem, positions, ffn1_norm, ffn1_w_gate_up, ffn1_w_down, mix_norm, mem_norm, w_mem_kv,
           mem_q_norm, mem_k_norm, w_out, ffn2_norm, ffn2_w_gate_up, ffn2_w_down, lru_w_in, lru_conv_w,
           lru_conv_b, lru_gate_w, lru_gate_b, lru_lambda, mla_w_in, mla_q_a_norm, mla_w_uq,
           mla_kv_a_norm, mla_w_ukv, mla_q_norm, mla_k_norm):
    batch, seq, d = x.shape
    depth = ffn1_norm.shape[0]
    t = batch * seq
    xf = x.reshape(t, d)
    tables = _rope_tables(positions)
    tk = _token_tile(seq)

    for layer in range(depth):
        j = layer // 2
        if layer % 2 == 0:
            w_in, splits = lru_w_in[j].astype(BF16), (LRU_WIDTH, LRU_WIDTH, MEM_WIDTH)
        else:
            w_in, splits = _mla_in_weight(mla_w_in[j]), (Q_LORA_RANK, KV_LORA_RANK, MEM_WIDTH, HEAD_PAD)
        xf, *u = _ffn(xf, ffn1_norm[layer], ffn1_w_gate_up[layer].astype(BF16), ffn1_w_down[layer].astype(BF16),
                      f"ffn1_in_{layer}", mix_norm[layer], w_in, splits)
        kt, v = _mem_kv(mem, mem_norm[layer], w_mem_kv[layer].astype(BF16), mem_k_norm[layer],
                        f"mem_kv_{layer}")
        if layer % 2 == 0:
            gate, xr, u_mem = u
            gw = jnp.stack([
                jnp.concatenate([_dense_block_diag(lru_gate_w[j, dr, 0]),
                                 _dense_block_diag(lru_gate_w[j, dr, 1])], axis=1)
                for dr in range(2)]).astype(BF16)
            h_dirs = _lru_scan(xr.reshape(batch, seq, LRU_WIDTH), lru_conv_w[j], lru_conv_b[j], gw,
                               lru_gate_b[j], lru_lambda[j], f"lru_scan_{layer}")
            toks = (h_dirs.reshape(2, t, LRU_WIDTH), gate)
        else:
            cq, ckv, u_mem, kr = u
            q_t, k, v_t = _mla_prep(cq, ckv, kr, tables, mla_q_a_norm[j], mla_kv_a_norm[j],
                                    mla_q_norm[j], mla_k_norm[j], mla_w_uq[j], mla_w_ukv[j],
                                    batch, seq, tk, f"mla_prep_{layer}")
            attn = _attention(q_t, k, v_t, tk, f"mla_attn_{layer}")
            toks = (attn.reshape(t, TOK_WIDTH),)
        xf = _mix_out(xf, toks, u_mem, mem_q_norm[layer], kt, v, w_out[layer].astype(BF16), seq,
                      f"mix_out_{layer}")
        xf = _ffn(xf, ffn2_norm[layer], ffn2_w_gate_up[layer].astype(BF16),
                  ffn2_w_down[layer].astype(BF16), f"ffn2_{layer}")
    return xf.reshape(batch, seq, d)
```

```python
import functools
import math

import jax
import jax.numpy as jnp
from jax import lax
from jax.experimental import pallas as pl
from jax.experimental.pallas import tpu as pltpu

F32 = jnp.float32
BF16 = jnp.bfloat16

D_MODEL = 1024
MEM_LEN = 256
MEM_HEADS = 4
MEM_HEAD_DIM = 64
MEM_WIDTH = MEM_HEADS * MEM_HEAD_DIM
TOK_WIDTH = D_MODEL - MEM_WIDTH
LRU_WIDTH = TOK_WIDTH
LRU_BLOCKS = 8
LRU_BLOCK_DIM = LRU_WIDTH // LRU_BLOCKS
CONV_WIDTH = 4
CONV_LEFT = 2
LRU_C = 8.0
MLA_HEADS = 12
QK_NOPE_DIM = 64
QK_ROPE_DIM = 32
QK_DIM = QK_NOPE_DIM + QK_ROPE_DIM
V_HEAD_DIM = TOK_WIDTH // MLA_HEADS
Q_LORA_RANK = 384
KV_LORA_RANK = 256
ROPE_THETA = 10000.0
D_FF = 2816
EPS = 1e-6

LANES = 128
SUBLANES = 8
HEAD_PAD = LANES
BF16_SUBLANES = 2 * SUBLANES
V_ROWS = V_HEAD_DIM + BF16_SUBLANES
ROPE_HALF = QK_ROPE_DIM // 2
VMEM_LIMIT = 56 * 1024 * 1024
LOG2E = math.log2(math.e)
ATTN_BLOCKS_PER_TRIP = 16
LAG_LIMIT = 64.0
ATTN_PAIRS_PER_TRIP = 1


def _params(sem, vmem=VMEM_LIMIT):
    return pltpu.CompilerParams(dimension_semantics=sem, vmem_limit_bytes=vmem)


def _rms(x, g):
    ms = jnp.mean(x * x, axis=-1, keepdims=True)
    return x * lax.rsqrt(ms + EPS) * g


def _full(shape):
    zeros = (0,) * len(shape)
    return pl.BlockSpec(shape, lambda *_: zeros)


def _token_tile(t):
    for tm in (512, 256, 128, 64, 32, 16, 8):
        if t % tm == 0:
            return tm
    raise ValueError(f"token count {t} must be a multiple of 8")


def _ffn_kernel(x_ref, g_ref, wgu_ref, wd_ref, *rest, splits):
    x = x_ref[...]
    h = _rms(x, g_ref[...]).astype(BF16)
    gu = jnp.dot(h, wgu_ref[...], preferred_element_type=F32)
    gate, up = gu[:, :D_FF], gu[:, D_FF:]
    act = (gate * jax.nn.sigmoid(gate) * up).astype(BF16)
    x_new = x + 0.5 * jnp.dot(act, wd_ref[...], preferred_element_type=F32)
    if not splits:
        (o_ref,) = rest
        o_ref[...] = x_new
        return
    g_mix_ref, w_in_ref, o_ref, *u_refs = rest
    o_ref[...] = x_new
    u = jnp.dot(_rms(x_new, g_mix_ref[...]).astype(BF16), w_in_ref[...], preferred_element_type=F32)
    off = 0
    for u_ref, n in zip(u_refs, splits):
        u_ref[...] = u[:, off:off + n]
        off += n


def _ffn(x, g, wgu, wd, name, g_mix=None, w_in=None, splits=()):
    t, d = x.shape
    tm = _token_tile(t)
    once = pl.Buffered(1)
    row = lambda n: pl.BlockSpec((tm, n), lambda i: (i, 0))
    resident = lambda w: pl.BlockSpec(w.shape, lambda i: (0, 0), pipeline_mode=once)
    in_specs = [row(d), _full((1, d)), resident(wgu), resident(wd)]
    args = [x, g.reshape(1, d), wgu, wd]
    if splits:
        assert sum(splits) == w_in.shape[1] and all(s % LANES == 0 for s in splits)
        in_specs += [_full((1, d)), resident(w_in)]
        args += [g_mix.reshape(1, d), w_in]
    outs = pl.pallas_call(
        functools.partial(_ffn_kernel, splits=tuple(splits)),
        grid=(t // tm,),
        in_specs=in_specs,
        out_specs=[row(d)] + [row(s) for s in splits],
        out_shape=[jax.ShapeDtypeStruct((t, d), F32)] + [jax.ShapeDtypeStruct((t, s), F32) for s in splits],
        compiler_params=_params(("parallel",)),
        name=name,
    )(*args)
    return outs if splits else outs[0]


def _head_rms_scale(x, head_dim):
    lane = lax.broadcasted_iota(jnp.int32, x.shape, 1)
    x2 = x * x
    scale = jnp.zeros_like(x)
    for h in range(x.shape[1] // head_dim):
        in_head = (lane >= h * head_dim) & (lane < (h + 1) * head_dim)
        ss = jnp.sum(jnp.where(in_head, x2, 0.0), axis=-1, keepdims=True)
        scale = jnp.where(in_head, lax.rsqrt(ss * (1.0 / head_dim) + EPS), scale)
    return scale


def _mem_kv_kernel(mem_ref, g_ref, w_ref, kg_ref, kt_ref, v_ref):
    mn = _rms(mem_ref[0], g_ref[...]).astype(BF16)
    kv = jnp.dot(mn, w_ref[...], preferred_element_type=F32)
    k, v = kv[:, :MEM_WIDTH], kv[:, MEM_WIDTH:]
    kn = k * _head_rms_scale(k, MEM_HEAD_DIM) * kg_ref[...]
    kt_ref[0] = kn.T.astype(BF16)
    v_ref[0] = v.astype(BF16)


def _mem_kv(mem, g, w, k_gain, name):
    b, m, d = mem.shape
    return pl.pallas_call(
        _mem_kv_kernel,
        grid=(b,),
        in_specs=[pl.BlockSpec((1, m, d), lambda i: (i, 0, 0)), _full((1, d)),
                  _full(w.shape), _full((1, MEM_WIDTH))],
        out_specs=[pl.BlockSpec((1, MEM_WIDTH, m), lambda i: (i, 0, 0)),
                   pl.BlockSpec((1, m, MEM_WIDTH), lambda i: (i, 0, 0))],
        out_shape=[jax.ShapeDtypeStruct((b, MEM_WIDTH, m), BF16),
                   jax.ShapeDtypeStruct((b, m, MEM_WIDTH), BF16)],
        compiler_params=_params(("parallel",)),
        name=name,
    )(mem, g.reshape(1, d), w, jnp.tile(k_gain, MEM_HEADS).reshape(1, MEM_WIDTH))


def _gelu_tanh(x):
    return 0.5 * x * (1.0 + jnp.tanh(math.sqrt(2.0 / math.pi) * (x + 0.044715 * (x * x * x))))


def _mem_attention(q, qg, kt, v):
    lane = lax.broadcasted_iota(jnp.int32, q.shape, 1)
    qn = q * _head_rms_scale(q, MEM_HEAD_DIM) * (qg * (MEM_HEAD_DIM ** -0.5))
    out = jnp.zeros_like(q)
    for h in range(MEM_HEADS):
        in_head = (lane >= h * MEM_HEAD_DIM) & (lane < (h + 1) * MEM_HEAD_DIM)
        qh = jnp.where(in_head, qn, 0.0).astype(BF16)
        s = jnp.dot(qh, kt, preferred_element_type=F32)
        p = jnp.exp(s - jnp.max(s, axis=-1, keepdims=True))
        denom = jnp.sum(p, axis=-1, keepdims=True)
        oh = jnp.dot(p.astype(BF16), v, preferred_element_type=F32)
        out = jnp.where(in_head, oh / denom, out)
    return out


def _mix_out_lru_kernel(x_ref, hf_ref, hr_ref, gate_ref, um_ref, qg_ref, kt_ref, v_ref,
                        wt_ref, wm_ref, o_ref):
    tok = (hf_ref[0] + hr_ref[0]) * _gelu_tanh(gate_ref[...])
    mem_out = _mem_attention(um_ref[...], qg_ref[...], kt_ref[0], v_ref[0])
    y = jnp.dot(tok.astype(BF16), wt_ref[...], preferred_element_type=F32)
    y = y + jnp.dot(mem_out.astype(BF16), wm_ref[...], preferred_element_type=F32)
    o_ref[...] = x_ref[...] + y


def _mix_out_mla_kernel(x_ref, tok_ref, um_ref, qg_ref, kt_ref, v_ref, wt_ref, wm_ref, o_ref):
    mem_out = _mem_attention(um_ref[...], qg_ref[...], kt_ref[0], v_ref[0])
    y = jnp.dot(tok_ref[...], wt_ref[...], preferred_element_type=F32)
    y = y + jnp.dot(mem_out.astype(BF16), wm_ref[...], preferred_element_type=F32)
    o_ref[...] = x_ref[...] + y


def _mix_out(x, toks, u_mem, q_gain, kt, v, w_out, seq, name):
    t, d = x.shape
    tm = _token_tile(seq)
    per_batch = seq // tm
    row = lambda w: pl.BlockSpec((tm, w), lambda i: (i, 0))
    per_b = lambda shape: pl.BlockSpec((1,) + shape, lambda i: (i // per_batch, 0, 0))
    common_specs = [row(MEM_WIDTH), _full((1, MEM_WIDTH)), per_b(kt.shape[1:]), per_b(v.shape[1:]),
                    _full((TOK_WIDTH, d)), _full((MEM_WIDTH, d))]
    common_args = (u_mem, jnp.tile(q_gain, MEM_HEADS).reshape(1, MEM_WIDTH), kt, v,
                   w_out[:TOK_WIDTH], w_out[TOK_WIDTH:])
    if len(toks) == 2:
        h_dirs, gate = toks
        kern = _mix_out_lru_kernel
        tok_specs = [pl.BlockSpec((1, tm, TOK_WIDTH), lambda i: (0, i, 0)),
                     pl.BlockSpec((1, tm, TOK_WIDTH), lambda i: (1, i, 0)), row(TOK_WIDTH)]
        tok_args = (h_dirs, h_dirs, gate)
    else:
        kern = _mix_out_mla_kernel
        tok_specs = [row(TOK_WIDTH)]
        tok_args = toks
    return pl.pallas_call(
        kern,
        grid=(t // tm,),
        in_specs=[row(d)] + tok_specs + common_specs,
        out_specs=row(d),
        out_shape=jax.ShapeDtypeStruct((t, d), F32),
        compiler_params=_params(("parallel",)),
        name=name,
    )(x, *tok_args, *common_args)


def _sigmoid(x):
    return 0.5 * jnp.tanh(0.5 * x) + 0.5


def _softplus(x):
    return jnp.maximum(x, 0.0) + jnp.log1p(jnp.exp(-jnp.abs(x)))


def _group_scan(a, b, reverse):
    row = lax.broadcasted_iota(jnp.int32, a.shape, 0)
    for dist in (1, 2, 4):
        if reverse:
            valid = row < SUBLANES - dist
            shift = SUBLANES - dist
        else:
            valid = row >= dist
            shift = dist
        a_sh = jnp.where(valid, pltpu.roll(a, shift, 0), 1.0)
        b_sh = jnp.where(valid, pltpu.roll(b, shift, 0), 0.0)
        b = a * b_sh + b
        a = a * a_sh
    return a, b


def _lru_kernel(cur_ref, prev_ref, next_ref, cw_ref, cb_ref, gw_ref, gb_ref, lam_ref, o_ref,
                ext_sc, a_sc, b_sc, h_sc, *, tc):
    direction = pl.program_id(0)
    step = pl.program_id(2)
    n_chunks = pl.num_programs(2)
    chunk = jnp.where(direction == 0, step, n_chunks - 1 - step)

    @pl.when(step == 0)
    def _():
        h_sc[...] = jnp.zeros_like(h_sc)

    ext_sc[0:SUBLANES, :] = jnp.where(chunk > 0, prev_ref[0], 0.0)
    ext_sc[SUBLANES:SUBLANES + tc, :] = cur_ref[0]
    ext_sc[SUBLANES + tc:, :] = jnp.where(chunk < n_chunks - 1, next_ref[0], 0.0)
    ext = ext_sc[...]
    xc = jnp.broadcast_to(cb_ref[...], (tc, LRU_WIDTH))
    for tap in range(CONV_WIDTH):
        shifted = pltpu.roll(ext, (CONV_LEFT - tap) % ext.shape[0], 0) if tap != CONV_LEFT else ext
        xc = xc + cw_ref[tap:tap + 1, :] * shifted[SUBLANES:SUBLANES + tc, :]

    z = jnp.dot(xc.astype(BF16), gw_ref[0], preferred_element_type=F32)
    r_gate = _sigmoid(z[:, :LRU_WIDTH] + gb_ref[0, 0:1, :])
    i_gate = _sigmoid(z[:, LRU_WIDTH:] + gb_ref[0, 1:2, :])
    log_a = (-LRU_C) * r_gate * _softplus(-lam_ref[0])
    a = jnp.exp(log_a)
    a_sc[...] = a
    b_sc[...] = jnp.sqrt(-jnp.tanh(log_a) * (a * a + 1.0)) * (i_gate * xc)

    n_groups = tc // SUBLANES

    def run(reverse):
        def body(g, h):
            grp = n_groups - 1 - g if reverse else g
            off = pl.multiple_of(grp * SUBLANES, SUBLANES)
            a, b = _group_scan(a_sc[pl.ds(off, SUBLANES), :], b_sc[pl.ds(off, SUBLANES), :], reverse)
            hs = a * h + b
            o_ref[0, 0, pl.ds(off, SUBLANES), :] = hs
            last = hs[0:1, :] if reverse else hs[SUBLANES - 1:SUBLANES, :]
            return jnp.broadcast_to(last, hs.shape)
        h_sc[...] = lax.fori_loop(0, n_groups, body, h_sc[...], unroll=4)

    @pl.when(direction == 0)
    def _():
        run(False)

    @pl.when(direction == 1)
    def _():
        run(True)


def _lru_scan(xr, conv_w, conv_b, gate_w, gate_b, lam, name):
    b, s, w = xr.shape
    tc = _token_tile(s)
    n_chunks = s // tc
    per_chunk = tc // SUBLANES
    n_rows8 = s // SUBLANES

    def chunk_of(d, i):
        return jnp.where(d == 0, i, n_chunks - 1 - i)

    return pl.pallas_call(
        functools.partial(_lru_kernel, tc=tc),
        grid=(2, b, n_chunks),
        in_specs=[
            pl.BlockSpec((1, tc, w), lambda d, bi, i: (bi, chunk_of(d, i), 0)),
            pl.BlockSpec((1, SUBLANES, w),
                         lambda d, bi, i: (bi, jnp.maximum(chunk_of(d, i) * per_chunk - 1, 0), 0)),
            pl.BlockSpec((1, SUBLANES, w),
                         lambda d, bi, i: (bi, jnp.minimum((chunk_of(d, i) + 1) * per_chunk, n_rows8 - 1), 0)),
            _full((CONV_WIDTH, w)),
            _full((1, w)),
            pl.BlockSpec((1, w, 2 * w), lambda d, bi, i: (d, 0, 0)),
            pl.BlockSpec((1, 2, w), lambda d, bi, i: (d, 0, 0)),
            pl.BlockSpec((1, 1, w), lambda d, bi, i: (d, 0, 0)),
        ],
        out_specs=pl.BlockSpec((1, 1, tc, w), lambda d, bi, i: (d, bi, chunk_of(d, i), 0)),
        out_shape=jax.ShapeDtypeStruct((2, b, s, w), F32),
        scratch_shapes=[pltpu.VMEM((tc + 2 * SUBLANES, w), F32), pltpu.VMEM((tc, w), F32),
                        pltpu.VMEM((tc, w), F32), pltpu.VMEM((SUBLANES, w), F32)],
        compiler_params=_params(("arbitrary", "arbitrary", "arbitrary")),
        name=name,
    )(xr, xr, xr, conv_w, conv_b.reshape(1, w), gate_w, gate_b, lam.reshape(2, 1, w))


def _dense_block_diag(blocks):
    nb, bw, _ = blocks.shape
    eye = jnp.eye(nb, dtype=blocks.dtype)
    return (eye[:, None, :, None] * blocks[:, :, None, :]).reshape(nb * bw, nb * bw)


ROPE_X1_LO = 0
ROPE_X2_LO = HEAD_PAD // 2
NOPE_A = (ROPE_HALF, HEAD_PAD // 2)
NOPE_B = (HEAD_PAD // 2 + ROPE_HALF, QK_DIM)
NOPE_SPLIT = NOPE_A[1] - NOPE_A[0]


def _slot_pad(nope, rope):
    ref = nope if nope is not None else rope
    lead = ref.shape[:-1]
    z = lambda n: jnp.zeros(lead + (n,), ref.dtype)
    nope = z(QK_NOPE_DIM) if nope is None else nope
    rope = z(QK_ROPE_DIM) if rope is None else rope
    return jnp.concatenate([rope[..., :ROPE_HALF], nope[..., :NOPE_SPLIT], rope[..., ROPE_HALF:],
                            nope[..., NOPE_SPLIT:], z(HEAD_PAD - QK_DIM)], axis=-1)


def _rope_table_kernel(pos_ref, freq_ref, cos_t_ref, sin_t_ref):
    ang = freq_ref[...] * pos_ref[...].astype(F32)
    cos_t_ref[...] = jnp.cos(ang)
    sin_t_ref[...] = jnp.sin(ang)


def _rope_tables(positions):
    t = positions.size
    tm = _token_tile(t)
    inv_freq = ROPE_THETA ** (-jnp.arange(ROPE_HALF, dtype=F32) * (2.0 / QK_ROPE_DIM))
    col = pl.BlockSpec((ROPE_HALF, tm), lambda i: (0, i))
    return pl.pallas_call(
        _rope_table_kernel,
        grid=(t // tm,),
        in_specs=[pl.BlockSpec((1, tm), lambda i: (0, i)), _full((ROPE_HALF, 1))],
        out_specs=[col, col],
        out_shape=[jax.ShapeDtypeStruct((ROPE_HALF, t), F32)] * 2,
        compiler_params=_params(("parallel",)),
        name="rope_tables",
    )(positions.reshape(1, t), inv_freq.reshape(ROPE_HALF, 1))


def _head_norm_rope_t(y, gain, cos_t, sin_t):
    yn = y * lax.rsqrt(jnp.sum(y * y, axis=0, keepdims=True) * (1.0 / QK_DIM) + EPS) * gain
    y1 = yn[ROPE_X1_LO:ROPE_X1_LO + ROPE_HALF, :]
    y2 = yn[ROPE_X2_LO:ROPE_X2_LO + ROPE_HALF, :]
    return jnp.concatenate([y1 * cos_t - y2 * sin_t, yn[NOPE_A[0]:NOPE_A[1], :],
                            y2 * cos_t + y1 * sin_t, yn[ROPE_X2_LO + ROPE_HALF:, :]], axis=0)


def _mla_prep_kernel(cq_ref, ckv_ref, kr_ref, cos_t_ref, sin_t_ref, qa_ref, kva_ref, qg_ref, kg_ref,
                     wq_t_ref, wk_t_ref, wv_t_ref, qt_ref, k_ref, vt_ref):
    tk = cq_ref.shape[0]
    cqn_t = _rms(cq_ref[...], qa_ref[...]).T.astype(BF16)
    ckn_t = _rms(ckv_ref[...], kva_ref[...]).T.astype(BF16)
    yq_t = jnp.dot(wq_t_ref[...], cqn_t, preferred_element_type=F32)
    yk_t = jnp.dot(wk_t_ref[...], ckn_t, preferred_element_type=F32)
    yv_t = jnp.dot(wv_t_ref[...], ckn_t, preferred_element_type=F32)
    k_rope_t = kr_ref[...].T
    cos_t, sin_t = cos_t_ref[...], sin_t_ref[...]
    q_gain = jnp.broadcast_to(qg_ref[...] * ((QK_DIM ** -0.5) * LOG2E), (HEAD_PAD, tk))
    k_gain = jnp.broadcast_to(kg_ref[...], (HEAD_PAD, tk))
    pad_row = lax.broadcasted_iota(jnp.int32, (BF16_SUBLANES, tk), 0)
    ones_rows = jnp.where(pad_row == 0, 1.0, 0.0)
    for h in range(MLA_HEADS):
        rows = slice(h * HEAD_PAD, (h + 1) * HEAD_PAD)
        qt_ref[0, h] = _head_norm_rope_t(yq_t[rows, :], q_gain, cos_t, sin_t).astype(BF16)
        k_t = _head_norm_rope_t(yk_t[rows, :] + k_rope_t, k_gain, cos_t, sin_t)
        k_ref[0, :, rows] = k_t.T.astype(BF16)
        vt_ref[0, h, 0] = jnp.concatenate(
            [yv_t[h * V_HEAD_DIM:(h + 1) * V_HEAD_DIM, :], ones_rows], axis=0).astype(BF16)


def _mla_prep(cq, ckv, kr, tables, qa, kva, qg, kg, w_uq, w_ukv, batch, seq, tk, name):
    n_kv = seq // tk
    row = lambda w: pl.BlockSpec((tk, w), lambda b, i: (b * n_kv + i, 0))
    col = pl.BlockSpec((ROPE_HALF, tk), lambda b, i: (0, b * n_kv + i))
    width = MLA_HEADS * HEAD_PAD
    wq = w_uq.reshape(Q_LORA_RANK, MLA_HEADS, QK_DIM)
    wq_t = _slot_pad(wq[..., :QK_NOPE_DIM], wq[..., QK_NOPE_DIM:]).reshape(Q_LORA_RANK, width).T.astype(BF16)
    wkv = w_ukv.reshape(KV_LORA_RANK, MLA_HEADS, QK_NOPE_DIM + V_HEAD_DIM)
    wk_t = _slot_pad(wkv[..., :QK_NOPE_DIM], None).reshape(KV_LORA_RANK, width).T.astype(BF16)
    wv_t = wkv[..., QK_NOPE_DIM:].reshape(KV_LORA_RANK, MLA_HEADS * V_HEAD_DIM).T.astype(BF16)
    qg_col = _slot_pad(qg[:QK_NOPE_DIM], qg[QK_NOPE_DIM:]).reshape(HEAD_PAD, 1)
    kg_col = _slot_pad(kg[:QK_NOPE_DIM], kg[QK_NOPE_DIM:]).reshape(HEAD_PAD, 1)
    cos_t, sin_t = tables
    return pl.pallas_call(
        _mla_prep_kernel,
        grid=(batch, n_kv),
        in_specs=[row(Q_LORA_RANK), row(KV_LORA_RANK), row(HEAD_PAD), col, col,
                  _full((1, Q_LORA_RANK)), _full((1, KV_LORA_RANK)), _full((HEAD_PAD, 1)), _full((HEAD_PAD, 1)),
                  _full(wq_t.shape), _full(wk_t.shape), _full(wv_t.shape)],
        out_specs=[pl.BlockSpec((1, MLA_HEADS, HEAD_PAD, tk), lambda b, i: (b, 0, 0, i)),
                   pl.BlockSpec((1, tk, width), lambda b, i: (b, i, 0)),
                   pl.BlockSpec((1, MLA_HEADS, 1, V_ROWS, tk), lambda b, i: (b, 0, i, 0, 0))],
        out_shape=[jax.ShapeDtypeStruct((batch, MLA_HEADS, HEAD_PAD, seq), BF16),
                   jax.ShapeDtypeStruct((batch, seq, width), BF16),
                   jax.ShapeDtypeStruct((batch, MLA_HEADS, n_kv, V_ROWS, tk), BF16)],
        compiler_params=_params(("parallel", "parallel")),
        name=name,
    )(cq, ckv, kr, cos_t, sin_t, qa.reshape(1, -1), kva.reshape(1, -1), qg_col, kg_col, wq_t, wk_t, wv_t)


def _attn_kernel(qt_ref, k_ref, vt_ref, o_ref, s_sc, acc_sc, *, tk, n_kv, pairs_per_trip, blocks_per_trip):
    tq = qt_ref.shape[-1]

    def k_block(j, hh):
        off = pl.multiple_of(j * tk, tk)
        return k_ref[0, pl.ds(off, tk), hh * HEAD_PAD:(hh + 1) * HEAD_PAD]

    def write_out():
        out_t = jnp.concatenate(
            [acc_sc[hh, :V_HEAD_DIM, :] / acc_sc[hh, V_HEAD_DIM:V_HEAD_DIM + 1, :] for hh in range(2)], axis=0)
        o_ref[0] = out_t.T.astype(o_ref.dtype)

    def qk(j, hh):
        return jnp.dot(k_block(j, hh), qt_ref[0, hh], preferred_element_type=F32)

    def finish(j, hh, s_t, m_ref, excess):
        p = jnp.exp2(s_t - m_ref).astype(BF16)
        pv = jnp.dot(vt_ref[0, hh, j], p, preferred_element_type=F32)
        rise = jnp.maximum(jnp.log2(jnp.max(p, axis=0, keepdims=True).astype(F32)), 0.0)
        acc_sc[hh] = (acc_sc[hh] + pv) * jnp.exp2(-rise)
        return m_ref + rise, jnp.maximum(excess, rise)

    def lagged_blocks(first, count, m_ref, excess):
        m_ref, excess = list(m_ref), list(excess)
        pending = None
        for u in range(count):
            for hh in range(2):
                s_t = qk(first + u, hh)
                if pending is not None:
                    pj, ph, ps = pending
                    m_ref[ph], excess[ph] = finish(pj, ph, ps, m_ref[ph], excess[ph])
                pending = (first + u, hh, s_t)
        pj, ph, ps = pending
        m_ref[ph], excess[ph] = finish(pj, ph, ps, m_ref[ph], excess[ph])
        return tuple(m_ref), tuple(excess)

    m_ref = tuple(
        jnp.max(jnp.dot(k_ref[0, 0:BF16_SUBLANES, hh * HEAD_PAD:(hh + 1) * HEAD_PAD], qt_ref[0, hh],
                        preferred_element_type=F32), axis=0, keepdims=True) for hh in range(2))
    acc_sc[...] = jnp.zeros_like(acc_sc)
    lag_state = (m_ref, tuple(jnp.zeros((1, tq), F32) for _ in range(2)))

    def lagged_trip(it, carry):
        return lagged_blocks(it * blocks_per_trip, blocks_per_trip, *carry)

    n_lag_trips = n_kv // blocks_per_trip
    lag_state = lax.fori_loop(0, n_lag_trips, lagged_trip, lag_state)
    n_left = n_kv - n_lag_trips * blocks_per_trip
    if n_left:
        lag_state = lagged_blocks(n_kv - n_left, n_left, *lag_state)
    write_out()
    worst_excess = jnp.max(jnp.maximum(lag_state[1][0], lag_state[1][1]))

    def scores(j, slot):
        col_max = []
        for hh in range(2):
            s_t = jnp.dot(k_block(j, hh), qt_ref[0, hh], preferred_element_type=F32)
            s_sc[slot, hh] = s_t
            col_max.append(jnp.max(s_t, axis=0, keepdims=True))
        return tuple(col_max)

    def consume(j, slot, col_max, m_run):
        new = []
        for hh in range(2):
            m_new = jnp.maximum(m_run[hh], col_max[hh])
            alpha = jnp.exp2(m_run[hh] - m_new)
            p = jnp.exp2(s_sc[slot, hh] - m_new)
            pv = jnp.dot(vt_ref[0, hh, j], p.astype(BF16), preferred_element_type=F32)
            acc_sc[hh] = alpha * acc_sc[hh] + pv
            new.append(m_new)
        return tuple(new)

    def pair(j, max_even, state):
        max_odd = scores(j + 1, 1)
        state = consume(j, 0, max_even, state)
        max_even = scores(j + 2, 0)
        state = consume(j + 1, 1, max_odd, state)
        return max_even, state

    def pair_trip(it, carry):
        for u in range(pairs_per_trip):
            carry = pair(2 * (it * pairs_per_trip + u), *carry)
        return carry

    @pl.when(worst_excess > LAG_LIMIT)
    def _():
        acc_sc[...] = jnp.zeros_like(acc_sc)
        state = tuple(jnp.full((1, tq), -jnp.inf, F32) for _ in range(2))
        max_even = scores(0, 0)
        n_full_pairs = n_kv // 2 - 1
        n_trips = n_full_pairs // pairs_per_trip
        max_even, state = lax.fori_loop(0, n_trips, pair_trip, (max_even, state))
        for p in range(n_trips * pairs_per_trip, n_full_pairs):
            max_even, state = pair(2 * p, max_even, state)
        max_odd = scores(n_kv - 1, 1)
        state = consume(n_kv - 2, 0, max_even, state)
        consume(n_kv - 1, 1, max_odd, state)
        write_out()


def _attention(q_t, k, v_t, tq, name):
    batch, heads, _, seq = q_t.shape
    n_kv, tk = v_t.shape[2], v_t.shape[4]
    assert n_kv % 2 == 0, "the kv loop is unrolled in pairs of blocks"
    pairs = heads // 2
    return pl.pallas_call(
        functools.partial(_attn_kernel, tk=tk, n_kv=n_kv, pairs_per_trip=ATTN_PAIRS_PER_TRIP,
                          blocks_per_trip=ATTN_BLOCKS_PER_TRIP),
        grid=(batch, pairs, seq // tq),
        in_specs=[pl.BlockSpec((1, 2, HEAD_PAD, tq), lambda b, hp, qi: (b, hp, 0, qi)),
                  pl.BlockSpec((1, seq, 2 * HEAD_PAD), lambda b, hp, qi: (b, 0, hp)),
                  pl.BlockSpec((1, 2, n_kv, V_ROWS, tk), lambda b, hp, qi: (b, hp, 0, 0, 0))],
        out_specs=pl.BlockSpec((1, tq, 2 * V_HEAD_DIM), lambda b, hp, qi: (b, qi, hp)),
        out_shape=jax.ShapeDtypeStruct((batch, seq, heads * V_HEAD_DIM), BF16),
        scratch_shapes=[pltpu.VMEM((2, 2, tk, tq), F32), pltpu.VMEM((2, V_ROWS, tq), F32)],
        compiler_params=_params(("parallel", "parallel", "arbitrary")),
        name=name,
    )(q_t, k, v_t)


def _mla_in_weight(w_in):
    c_q = w_in[:, :Q_LORA_RANK]
    c_kv = w_in[:, Q_LORA_RANK:Q_LORA_RANK + KV_LORA_RANK]
    k_rope = w_in[:, Q_LORA_RANK + KV_LORA_RANK:Q_LORA_RANK + KV_LORA_RANK + QK_ROPE_DIM]
    u_mem = w_in[:, Q_LORA_RANK + KV_LORA_RANK + QK_ROPE_DIM:]
    return jnp.concatenate([c_q, c_kv, u_mem, _slot_pad(None, k_rope)], axis=1).astype(BF16)


def kernel(x, mem, positions, ffn1_norm, ffn1_w_gate_up, ffn1_w_down, mix_norm, mem_norm, w_mem_kv,
           mem_q_norm, mem_k_norm, w_out, ffn2_norm, ffn2_w_gate_up, ffn2_w_down, lru_w_in, lru_conv_w,
           lru_conv_b, lru_gate_w, lru_gate_b, lru_lambda, mla_w_in, mla_q_a_norm, mla_w_uq,
           mla_kv_a_norm, mla_w_ukv, mla_q_norm, mla_k_norm):
    batch, seq, d = x.shape
    depth = ffn1_norm.shape[0]
    t = batch * seq
    xf = x.reshape(t, d)
    tables = _rope_tables(positions)
    tk = _token_tile(seq)

    for layer in range(depth):
        j = layer // 2
        if layer % 2 == 0:
            w_in, splits = lru_w_in[j].astype(BF16), (LRU_WIDTH, LRU_WIDTH, MEM_WIDTH)
        else:
            w_in, splits = _mla_in_weight(mla_w_in[j]), (Q_LORA_RANK, KV_LORA_RANK, MEM_WIDTH, HEAD_PAD)
        xf, *u = _ffn(xf, ffn1_norm[layer], ffn1_w_gate_up[layer].astype(BF16), ffn1_w_down[layer].astype(BF16),
                      f"ffn1_in_{layer}", mix_norm[layer], w_in, splits)
        kt, v = _mem_kv(mem, mem_norm[layer], w_mem_kv[layer].astype(BF16), mem_k_norm[layer],
                        f"mem_kv_{layer}")
        if layer % 2 == 0:
            gate, xr, u_mem = u
            gw = jnp.stack([
                jnp.concatenate([_dense_block_diag(lru_gate_w[j, dr, 0]),
                                 _dense_block_diag(lru_gate_w[j, dr, 1])], axis=1)
                for dr in range(2)]).astype(BF16)
            h_dirs = _lru_scan(xr.reshape(batch, seq, LRU_WIDTH), lru_conv_w[j], lru_conv_b[j], gw,
                               lru_gate_b[j], lru_lambda[j], f"lru_scan_{layer}")
            toks = (h_dirs.reshape(2, t, LRU_WIDTH), gate)
        else:
            cq, ckv, u_mem, kr = u
            q_t, k, v_t = _mla_prep(cq, ckv, kr, tables, mla_q_a_norm[j], mla_kv_a_norm[j],
                                    mla_q_norm[j], mla_k_norm[j], mla_w_uq[j], mla_w_ukv[j],
                                    batch, seq, tk, f"mla_prep_{layer}")
            attn = _attention(q_t, k, v_t, tk, f"mla_attn_{layer}")
            toks = (attn.reshape(t, TOK_WIDTH),)
        xf = _mix_out(xf, toks, u_mem, mem_q_norm[layer], kt, v, w_out[layer].astype(BF16), seq,
                      f"mix_out_{layer}")
        xf = _ffn(xf, ffn2_norm[layer], ffn2_w_gate_up[layer].astype(BF16),
                  ffn2_w_down[layer].astype(BF16), f"ffn2_{layer}")
    return xf.reshape(batch, seq, d)
```

```python
import functools
import math

import jax
import jax.numpy as jnp
from jax import lax
from jax.experimental import pallas as pl
from jax.experimental.pallas import tpu as pltpu

F32 = jnp.float32
BF16 = jnp.bfloat16

D_MODEL = 1024
MEM_LEN = 256
MEM_HEADS = 4
MEM_HEAD_DIM = 64
MEM_WIDTH = MEM_HEADS * MEM_HEAD_DIM
TOK_WIDTH = D_MODEL - MEM_WIDTH
LRU_WIDTH = TOK_WIDTH
LRU_BLOCKS = 8
LRU_BLOCK_DIM = LRU_WIDTH // LRU_BLOCKS
CONV_WIDTH = 4
CONV_LEFT = 2
LRU_C = 8.0
MLA_HEADS = 12
QK_NOPE_DIM = 64
QK_ROPE_DIM = 32
QK_DIM = QK_NOPE_DIM + QK_ROPE_DIM
V_HEAD_DIM = TOK_WIDTH // MLA_HEADS
Q_LORA_RANK = 384
KV_LORA_RANK = 256
ROPE_THETA = 10000.0
D_FF = 2816
EPS = 1e-6

LANES = 128
SUBLANES = 8
HEAD_PAD = LANES
BF16_SUBLANES = 2 * SUBLANES
V_ROWS = V_HEAD_DIM + BF16_SUBLANES
ROPE_HALF = QK_ROPE_DIM // 2
VMEM_LIMIT = 56 * 1024 * 1024
LOG2E = math.log2(math.e)
ATTN_BLOCKS_PER_TRIP = 16
LAG_LIMIT = 64.0
ATTN_PAIRS_PER_TRIP = 1


def _params(sem, vmem=VMEM_LIMIT):
    return pltpu.CompilerParams(dimension_semantics=sem, vmem_limit_bytes=vmem)


def _rms(x, g):
    ms = jnp.mean(x * x, axis=-1, keepdims=True)
    return x * lax.rsqrt(ms + EPS) * g


def _full(shape):
    zeros = (0,) * len(shape)
    return pl.BlockSpec(shape, lambda *_: zeros)


def _token_tile(t):
    for tm in (512, 256, 128, 64, 32, 16, 8):
        if t % tm == 0:
            return tm
    raise ValueError(f"token count {t} must be a multiple of 8")


def _ffn_kernel(x_ref, g_ref, wgu_ref, wd_ref, *rest, splits):
    x = x_ref[...]
    h = _rms(x, g_ref[...]).astype(BF16)
    gu = jnp.dot(h, wgu_ref[...], preferred_element_type=F32)
    gate, up = gu[:, :D_FF], gu[:, D_FF:]
    act = (gate * jax.nn.sigmoid(gate) * up).astype(BF16)
    x_new = x + 0.5 * jnp.dot(act, wd_ref[...], preferred_element_type=F32)
    if not splits:
        (o_ref,) = rest
        o_ref[...] = x_new
        return
    g_mix_ref, w_in_ref, o_ref, *u_refs = rest
    o_ref[...] = x_new
    u = jnp.dot(_rms(x_new, g_mix_ref[...]).astype(BF16), w_in_ref[...], preferred_element_type=F32)
    off = 0
    for u_ref, n in zip(u_refs, splits):
        u_ref[...] = u[:, off:off + n]
        off += n


def _ffn(x, g, wgu, wd, name, g_mix=None, w_in=None, splits=()):
    t, d = x.shape
    tm = _token_tile(t)
    once = pl.Buffered(1)
    row = lambda n: pl.BlockSpec((tm, n), lambda i: (i, 0))
    resident = lambda w: pl.BlockSpec(w.shape, lambda i: (0, 0), pipeline_mode=once)
    in_specs = [row(d), _full((1, d)), resident(wgu), resident(wd)]
    args = [x, g.reshape(1, d), wgu, wd]
    if splits:
        assert sum(splits) == w_in.shape[1] and all(s % LANES == 0 for s in splits)
        in_specs += [_full((1, d)), resident(w_in)]
        args += [g_mix.reshape(1, d), w_in]
    outs = pl.pallas_call(
        functools.partial(_ffn_kernel, splits=tuple(splits)),
        grid=(t // tm,),
        in_specs=in_specs,
        out_specs=[row(d)] + [row(s) for s in splits],
        out_shape=[jax.ShapeDtypeStruct((t, d), F32)] + [jax.ShapeDtypeStruct((t, s), F32) for s in splits],
        compiler_params=_params(("parallel",)),
        name=name,
    )(*args)
    return outs if splits else outs[0]


def _head_rms_scale(x, head_dim):
    lane = lax.broadcasted_iota(jnp.int32, x.shape, 1)
    x2 = x * x
    scale = jnp.zeros_like(x)
    for h in range(x.shape[1] // head_dim):
        in_head = (lane >= h * head_dim) & (lane < (h + 1) * head_dim)
        ss = jnp.sum(jnp.where(in_head, x2, 0.0), axis=-1, keepdims=True)
        scale = jnp.where(in_head, lax.rsqrt(ss * (1.0 / head_dim) + EPS), scale)
    return scale


def _mem_kv_kernel(mem_ref, g_ref, w_ref, kg_ref, kt_ref, v_ref):
    mn = _rms(mem_ref[0], g_ref[...]).astype(BF16)
    kv = jnp.dot(mn, w_ref[...], preferred_element_type=F32)
    k, v = kv[:, :MEM_WIDTH], kv[:, MEM_WIDTH:]
    kn = k * _head_rms_scale(k, MEM_HEAD_DIM) * kg_ref[...]
    kt_ref[0] = kn.T.astype(BF16)
    v_ref[0] = v.astype(BF16)


def _mem_kv(mem, g, w, k_gain, name):
    b, m, d = mem.shape
    return pl.pallas_call(
        _mem_kv_kernel,
        grid=(b,),
        in_specs=[pl.BlockSpec((1, m, d), lambda i: (i, 0, 0)), _full((1, d)),
                  _full(w.shape), _full((1, MEM_WIDTH))],
        out_specs=[pl.BlockSpec((1, MEM_WIDTH, m), lambda i: (i, 0, 0)),
                   pl.BlockSpec((1, m, MEM_WIDTH), lambda i: (i, 0, 0))],
        out_shape=[jax.ShapeDtypeStruct((b, MEM_WIDTH, m), BF16),
                   jax.ShapeDtypeStruct((b, m, MEM_WIDTH), BF16)],
        compiler_params=_params(("parallel",)),
        name=name,
    )(mem, g.reshape(1, d), w, jnp.tile(k_gain, MEM_HEADS).reshape(1, MEM_WIDTH))


def _gelu_tanh(x):
    return 0.5 * x * (1.0 + jnp.tanh(math.sqrt(2.0 / math.pi) * (x + 0.044715 * (x * x * x))))


def _mem_attention(q, qg, kt, v):
    lane = lax.broadcasted_iota(jnp.int32, q.shape, 1)
    qn = q * _head_rms_scale(q, MEM_HEAD_DIM) * (qg * (MEM_HEAD_DIM ** -0.5))
    out = jnp.zeros_like(q)
    for h in range(MEM_HEADS):
        in_head = (lane >= h * MEM_HEAD_DIM) & (lane < (h + 1) * MEM_HEAD_DIM)
        qh = jnp.where(in_head, qn, 0.0).astype(BF16)
        s = jnp.dot(qh, kt, preferred_element_type=F32)
        p = jnp.exp(s - jnp.max(s, axis=-1, keepdims=True))
        denom = jnp.sum(p, axis=-1, keepdims=True)
        oh = jnp.dot(p.astype(BF16), v, preferred_element_type=F32)
        out = jnp.where(in_head, oh / denom, out)
    return out


def _mix_out_lru_kernel(x_ref, hf_ref, hr_ref, gate_ref, um_ref, qg_ref, kt_ref, v_ref,
                        wt_ref, wm_ref, o_ref):
    tok = (hf_ref[0] + hr_ref[0]) * _gelu_tanh(gate_ref[...])
    mem_out = _mem_attention(um_ref[...], qg_ref[...], kt_ref[0], v_ref[0])
    y = jnp.dot(tok.astype(BF16), wt_ref[...], preferred_element_type=F32)
    y = y + jnp.dot(mem_out.astype(BF16), wm_ref[...], preferred_element_type=F32)
    o_ref[...] = x_ref[...] + y


def _mix_out_mla_kernel(x_ref, tok_ref, um_ref, qg_ref, kt_ref, v_ref, wt_ref, wm_ref, o_ref):
    mem_out = _mem_attention(um_ref[...], qg_ref[...], kt_ref[0], v_ref[0])
    y = jnp.dot(tok_ref[...], wt_ref[...], preferred_element_type=F32)
    y = y + jnp.dot(mem_out.astype(BF16), wm_ref[...], preferred_element_type=F32)
    o_ref[...] = x_ref[...] + y


def _mix_out(x, toks, u_mem, q_gain, kt, v, w_out, seq, name):
    t, d = x.shape
    tm = _token_tile(seq)
    per_batch = seq // tm
    row = lambda w: pl.BlockSpec((tm, w), lambda i: (i, 0))
    per_b = lambda shape: pl.BlockSpec((1,) + shape, lambda i: (i // per_batch, 0, 0))
    common_specs = [row(MEM_WIDTH), _full((1, MEM_WIDTH)), per_b(kt.shape[1:]), per_b(v.shape[1:]),
                    _full((TOK_WIDTH, d)), _full((MEM_WIDTH, d))]
    common_args = (u_mem, jnp.tile(q_gain, MEM_HEADS).reshape(1, MEM_WIDTH), kt, v,
                   w_out[:TOK_WIDTH], w_out[TOK_WIDTH:])
    if len(toks) == 2:
        h_dirs, gate = toks
        kern = _mix_out_lru_kernel
        tok_specs = [pl.BlockSpec((1, tm, TOK_WIDTH), lambda i: (0, i, 0)),
                     pl.BlockSpec((1, tm, TOK_WIDTH), lambda i: (1, i, 0)), row(TOK_WIDTH)]
        tok_args = (h_dirs, h_dirs, gate)
    else:
        kern = _mix_out_mla_kernel
        tok_specs = [row(TOK_WIDTH)]
        tok_args = toks
    return pl.pallas_call(
        kern,
        grid=(t // tm,),
        in_specs=[row(d)] + tok_specs + common_specs,
        out_specs=row(d),
        out_shape=jax.ShapeDtypeStruct((t, d), F32),
        compiler_params=_params(("parallel",)),
        name=name,
    )(x, *tok_args, *common_args)


def _sigmoid(x):
    return 0.5 * jnp.tanh(0.5 * x) + 0.5


def _softplus(x):
    return jnp.maximum(x, 0.0) + jnp.log1p(jnp.exp(-jnp.abs(x)))


def _group_scan(a, b, reverse):
    row = lax.broadcasted_iota(jnp.int32, a.shape, 0)
    for dist in (1, 2, 4):
        if reverse:
            valid = row < SUBLANES - dist
            shift = SUBLANES - dist
        else:
            valid = row >= dist
            shift = dist
        a_sh = jnp.where(valid, pltpu.roll(a, shift, 0), 1.0)
        b_sh = jnp.where(valid, pltpu.roll(b, shift, 0), 0.0)
        b = a * b_sh + b
        a = a * a_sh
    return a, b


def _lru_kernel(cur_ref, prev_ref, next_ref, cw_ref, cb_ref, gw_ref, gb_ref, lam_ref, o_ref,
                ext_sc, a_sc, b_sc, out_sc, h_sc, *, tc):
    direction = pl.program_id(0)
    step = pl.program_id(2)
    n_chunks = pl.num_programs(2)
    chunk = jnp.where(direction == 0, step, n_chunks - 1 - step)

    @pl.when(step == 0)
    def _():
        h_sc[...] = jnp.zeros_like(h_sc)

    ext_sc[0:SUBLANES, :] = jnp.where(chunk > 0, prev_ref[0], 0.0)
    ext_sc[SUBLANES:SUBLANES + tc, :] = cur_ref[0]
    ext_sc[SUBLANES + tc:, :] = jnp.where(chunk < n_chunks - 1, next_ref[0], 0.0)
    ext = ext_sc[...]
    xc = jnp.broadcast_to(cb_ref[...], (tc, LRU_WIDTH))
    for tap in range(CONV_WIDTH):
        shifted = pltpu.roll(ext, (CONV_LEFT - tap) % ext.shape[0], 0) if tap != CONV_LEFT else ext
        xc = xc + cw_ref[tap:tap + 1, :] * shifted[SUBLANES:SUBLANES + tc, :]

    z = jnp.dot(xc.astype(BF16), gw_ref[0], preferred_element_type=F32)
    r_gate = _sigmoid(z[:, :LRU_WIDTH] + gb_ref[0, 0:1, :])
    i_gate = _sigmoid(z[:, LRU_WIDTH:] + gb_ref[0, 1:2, :])
    log_a = (-LRU_C) * r_gate * _softplus(-lam_ref[0])
    a = jnp.exp(log_a)
    b = jnp.sqrt(-jnp.tanh(log_a) * (a * a + 1.0)) * (i_gate * xc)
    n_slabs = LRU_WIDTH // LANES
    for c in range(n_slabs):
        a_sc[c] = a[:, c * LANES:(c + 1) * LANES]
        b_sc[c] = b[:, c * LANES:(c + 1) * LANES]

    block = SUBLANES * SUBLANES
    n_blocks = tc // block
    row = lax.broadcasted_iota(jnp.int32, (SUBLANES, LANES), 0)

    def run(reverse):
        order = range(SUBLANES - 1, -1, -1) if reverse else range(SUBLANES)
        edge = SUBLANES - 1 if reverse else 0

        def body(i, h):
            base = ((n_blocks - 1 - i) if reverse else i) * block
            new_h = []
            for c in range(n_slabs):
                hs, ps = [None] * SUBLANES, [None] * SUBLANES
                h_loc = p_loc = None
                for r in order:
                    a_r = a_sc[c, pl.ds(base + r, SUBLANES, stride=SUBLANES), :]
                    b_r = b_sc[c, pl.ds(base + r, SUBLANES, stride=SUBLANES), :]
                    h_loc = b_r if h_loc is None else a_r * h_loc + b_r
                    p_loc = a_r if p_loc is None else a_r * p_loc
                    hs[r], ps[r] = h_loc, p_loc
                a_cum, b_cum = _group_scan(p_loc, h_loc, reverse)
                state = a_cum * h[c] + b_cum
                carry = jnp.where(row == edge, h[c], pltpu.roll(state, SUBLANES - 1 if reverse else 1, 0))
                for r in range(SUBLANES):
                    out_sc[c, pl.ds(base + r, SUBLANES, stride=SUBLANES), :] = hs[r] + ps[r] * carry
                last = state[0:1, :] if reverse else state[SUBLANES - 1:SUBLANES, :]
                new_h.append(jnp.broadcast_to(last, state.shape))
            return tuple(new_h)

        h0 = tuple(h_sc[:, c * LANES:(c + 1) * LANES] for c in range(n_slabs))
        h_end = lax.fori_loop(0, n_blocks, body, h0, unroll=2)
        for c in range(n_slabs):
            h_sc[:, c * LANES:(c + 1) * LANES] = h_end[c]
            o_ref[0, 0, :, c * LANES:(c + 1) * LANES] = out_sc[c]

    @pl.when(direction == 0)
    def _():
        run(False)

    @pl.when(direction == 1)
    def _():
        run(True)


def _lru_scan(xr, conv_w, conv_b, gate_w, gate_b, lam, name):
    b, s, w = xr.shape
    tc = _token_tile(s)
    n_chunks = s // tc
    per_chunk = tc // SUBLANES
    n_rows8 = s // SUBLANES

    def chunk_of(d, i):
        return jnp.where(d == 0, i, n_chunks - 1 - i)

    return pl.pallas_call(
        functools.partial(_lru_kernel, tc=tc),
        grid=(2, b, n_chunks),
        in_specs=[
            pl.BlockSpec((1, tc, w), lambda d, bi, i: (bi, chunk_of(d, i), 0)),
            pl.BlockSpec((1, SUBLANES, w),
                         lambda d, bi, i: (bi, jnp.maximum(chunk_of(d, i) * per_chunk - 1, 0), 0)),
            pl.BlockSpec((1, SUBLANES, w),
                         lambda d, bi, i: (bi, jnp.minimum((chunk_of(d, i) + 1) * per_chunk, n_rows8 - 1), 0)),
            _full((CONV_WIDTH, w)),
            _full((1, w)),
            pl.BlockSpec((1, w, 2 * w), lambda d, bi, i: (d, 0, 0)),
            pl.BlockSpec((1, 2, w), lambda d, bi, i: (d, 0, 0)),
            pl.BlockSpec((1, 1, w), lambda d, bi, i: (d, 0, 0)),
        ],
        out_specs=pl.BlockSpec((1, 1, tc, w), lambda d, bi, i: (d, bi, chunk_of(d, i), 0)),
        out_shape=jax.ShapeDtypeStruct((2, b, s, w), F32),
        scratch_shapes=[pltpu.VMEM((tc + 2 * SUBLANES, w), F32)]
        + [pltpu.VMEM((w // LANES, tc, LANES), F32)] * 3 + [pltpu.VMEM((SUBLANES, w), F32)],
        compiler_params=_params(("arbitrary", "arbitrary", "arbitrary")),
        name=name,
    )(xr, xr, xr, conv_w, conv_b.reshape(1, w), gate_w, gate_b, lam.reshape(2, 1, w))


def _dense_block_diag(blocks):
    nb, bw, _ = blocks.shape
    eye = jnp.eye(nb, dtype=blocks.dtype)
    return (eye[:, None, :, None] * blocks[:, :, None, :]).reshape(nb * bw, nb * bw)


ROPE_X1_LO = 0
ROPE_X2_LO = HEAD_PAD // 2
NOPE_A = (ROPE_HALF, HEAD_PAD // 2)
NOPE_B = (HEAD_PAD // 2 + ROPE_HALF, QK_DIM)
NOPE_SPLIT = NOPE_A[1] - NOPE_A[0]


def _slot_pad(nope, rope):
    ref = nope if nope is not None else rope
    lead = ref.shape[:-1]
    z = lambda n: jnp.zeros(lead + (n,), ref.dtype)
    nope = z(QK_NOPE_DIM) if nope is None else nope
    rope = z(QK_ROPE_DIM) if rope is None else rope
    return jnp.concatenate([rope[..., :ROPE_HALF], nope[..., :NOPE_SPLIT], rope[..., ROPE_HALF:],
                            nope[..., NOPE_SPLIT:], z(HEAD_PAD - QK_DIM)], axis=-1)


def _rope_table_kernel(pos_ref, freq_ref, cos_t_ref, sin_t_ref):
    ang = freq_ref[...] * pos_ref[...].astype(F32)
    cos_t_ref[...] = jnp.cos(ang)
    sin_t_ref[...] = jnp.sin(ang)


def _rope_tables(positions):
    t = positions.size
    tm = _token_tile(t)
    inv_freq = ROPE_THETA ** (-jnp.arange(ROPE_HALF, dtype=F32) * (2.0 / QK_ROPE_DIM))
    col = pl.BlockSpec((ROPE_HALF, tm), lambda i: (0, i))
    return pl.pallas_call(
        _rope_table_kernel,
        grid=(t // tm,),
        in_specs=[pl.BlockSpec((1, tm), lambda i: (0, i)), _full((ROPE_HALF, 1))],
        out_specs=[col, col],
        out_shape=[jax.ShapeDtypeStruct((ROPE_HALF, t), F32)] * 2,
        compiler_params=_params(("parallel",)),
        name="rope_tables",
    )(positions.reshape(1, t), inv_freq.reshape(ROPE_HALF, 1))


def _head_norm_rope_t(y, gain, cos_t, sin_t):
    yn = y * lax.rsqrt(jnp.sum(y * y, axis=0, keepdims=True) * (1.0 / QK_DIM) + EPS) * gain
    y1 = yn[ROPE_X1_LO:ROPE_X1_LO + ROPE_HALF, :]
    y2 = yn[ROPE_X2_LO:ROPE_X2_LO + ROPE_HALF, :]
    return jnp.concatenate([y1 * cos_t - y2 * sin_t, yn[NOPE_A[0]:NOPE_A[1], :],
                            y2 * cos_t + y1 * sin_t, yn[ROPE_X2_LO + ROPE_HALF:, :]], axis=0)


def _mla_prep_kernel(cq_ref, ckv_ref, kr_ref, cos_t_ref, sin_t_ref, qa_ref, kva_ref, qg_ref, kg_ref,
                     wq_t_ref, wk_t_ref, wv_t_ref, qt_ref, k_ref, vt_ref):
    tk = cq_ref.shape[0]
    cqn_t = _rms(cq_ref[...], qa_ref[...]).T.astype(BF16)
    ckn_t = _rms(ckv_ref[...], kva_ref[...]).T.astype(BF16)
    yq_t = jnp.dot(wq_t_ref[...], cqn_t, preferred_element_type=F32)
    yk_t = jnp.dot(wk_t_ref[...], ckn_t, preferred_element_type=F32)
    yv_t = jnp.dot(wv_t_ref[...], ckn_t, preferred_element_type=F32)
    k_rope_t = kr_ref[...].T
    cos_t, sin_t = cos_t_ref[...], sin_t_ref[...]
    q_gain = jnp.broadcast_to(qg_ref[...] * ((QK_DIM ** -0.5) * LOG2E), (HEAD_PAD, tk))
    k_gain = jnp.broadcast_to(kg_ref[...], (HEAD_PAD, tk))
    pad_row = lax.broadcasted_iota(jnp.int32, (BF16_SUBLANES, tk), 0)
    ones_rows = jnp.where(pad_row == 0, 1.0, 0.0)
    for h in range(MLA_HEADS):
        rows = slice(h * HEAD_PAD, (h + 1) * HEAD_PAD)
        qt_ref[0, h] = _head_norm_rope_t(yq_t[rows, :], q_gain, cos_t, sin_t).astype(BF16)
        k_t = _head_norm_rope_t(yk_t[rows, :] + k_rope_t, k_gain, cos_t, sin_t)
        k_ref[0, :, rows] = k_t.T.astype(BF16)
        vt_ref[0, h, 0] = jnp.concatenate(
            [yv_t[h * V_HEAD_DIM:(h + 1) * V_HEAD_DIM, :], ones_rows], axis=0).astype(BF16)


def _mla_prep(cq, ckv, kr, tables, qa, kva, qg, kg, w_uq, w_ukv, batch, seq, tk, name):
    n_kv = seq // tk
    row = lambda w: pl.BlockSpec((tk, w), lambda b, i: (b * n_kv + i, 0))
    col = pl.BlockSpec((ROPE_HALF, tk), lambda b, i: (0, b * n_kv + i))
    width = MLA_HEADS * HEAD_PAD
    wq = w_uq.reshape(Q_LORA_RANK, MLA_HEADS, QK_DIM)
    wq_t = _slot_pad(wq[..., :QK_NOPE_DIM], wq[..., QK_NOPE_DIM:]).reshape(Q_LORA_RANK, width).T.astype(BF16)
    wkv = w_ukv.reshape(KV_LORA_RANK, MLA_HEADS, QK_NOPE_DIM + V_HEAD_DIM)
    wk_t = _slot_pad(wkv[..., :QK_NOPE_DIM], None).reshape(KV_LORA_RANK, width).T.astype(BF16)
    wv_t = wkv[..., QK_NOPE_DIM:].reshape(KV_LORA_RANK, MLA_HEADS * V_HEAD_DIM).T.astype(BF16)
    qg_col = _slot_pad(qg[:QK_NOPE_DIM], qg[QK_NOPE_DIM:]).reshape(HEAD_PAD, 1)
    kg_col = _slot_pad(kg[:QK_NOPE_DIM], kg[QK_NOPE_DIM:]).reshape(HEAD_PAD, 1)
    cos_t, sin_t = tables
    return pl.pallas_call(
        _mla_prep_kernel,
        grid=(batch, n_kv),
        in_specs=[row(Q_LORA_RANK), row(KV_LORA_RANK), row(HEAD_PAD), col, col,
                  _full((1, Q_LORA_RANK)), _full((1, KV_LORA_RANK)), _full((HEAD_PAD, 1)), _full((HEAD_PAD, 1)),
                  _full(wq_t.shape), _full(wk_t.shape), _full(wv_t.shape)],
        out_specs=[pl.BlockSpec((1, MLA_HEADS, HEAD_PAD, tk), lambda b, i: (b, 0, 0, i)),
                   pl.BlockSpec((1, tk, width), lambda b, i: (b, i, 0)),
                   pl.BlockSpec((1, MLA_HEADS, 1, V_ROWS, tk), lambda b, i: (b, 0, i, 0, 0))],
        out_shape=[jax.ShapeDtypeStruct((batch, MLA_HEADS, HEAD_PAD, seq), BF16),
                   jax.ShapeDtypeStruct((batch, seq, width), BF16),
                   jax.ShapeDtypeStruct((batch, MLA_HEADS, n_kv, V_ROWS, tk), BF16)],
        compiler_params=_params(("parallel", "parallel")),
        name=name,
    )(cq, ckv, kr, cos_t, sin_t, qa.reshape(1, -1), kva.reshape(1, -1), qg_col, kg_col, wq_t, wk_t, wv_t)


def _attn_kernel(qt_ref, k_ref, vt_ref, o_ref, s_sc, acc_sc, *, tk, n_kv, pairs_per_trip, blocks_per_trip):
    tq = qt_ref.shape[-1]

    def k_block(j, hh):
        off = pl.multiple_of(j * tk, tk)
        return k_ref[0, pl.ds(off, tk), hh * HEAD_PAD:(hh + 1) * HEAD_PAD]

    def write_out():
        out_t = jnp.concatenate(
            [acc_sc[hh, :V_HEAD_DIM, :] / acc_sc[hh, V_HEAD_DIM:V_HEAD_DIM + 1, :] for hh in range(2)], axis=0)
        o_ref[0] = out_t.T.astype(o_ref.dtype)

    def qk(j, hh):
        return jnp.dot(k_block(j, hh), qt_ref[0, hh], preferred_element_type=F32)

    def finish(j, hh, s_t, m_ref, excess):
        p = jnp.exp2(s_t - m_ref).astype(BF16)
        pv = jnp.dot(vt_ref[0, hh, j], p, preferred_element_type=F32)
        rise = jnp.maximum(jnp.log2(jnp.max(p, axis=0, keepdims=True).astype(F32)), 0.0)
        acc_sc[hh] = (acc_sc[hh] + pv) * jnp.exp2(-rise)
        return m_ref + rise, jnp.maximum(excess, rise)

    def lagged_blocks(first, count, m_ref, excess):
        m_ref, excess = list(m_ref), list(excess)
        pending = None
        for u in range(count):
            for hh in range(2):
                s_t = qk(first + u, hh)
                if pending is not None:
                    pj, ph, ps = pending
                    m_ref[ph], excess[ph] = finish(pj, ph, ps, m_ref[ph], excess[ph])
                pending = (first + u, hh, s_t)
        pj, ph, ps = pending
        m_ref[ph], excess[ph] = finish(pj, ph, ps, m_ref[ph], excess[ph])
        return tuple(m_ref), tuple(excess)

    m_ref = tuple(
        jnp.max(jnp.dot(k_ref[0, 0:BF16_SUBLANES, hh * HEAD_PAD:(hh + 1) * HEAD_PAD], qt_ref[0, hh],
                        preferred_element_type=F32), axis=0, keepdims=True) for hh in range(2))
    acc_sc[...] = jnp.zeros_like(acc_sc)
    lag_state = (m_ref, tuple(jnp.zeros((1, tq), F32) for _ in range(2)))

    def lagged_trip(it, carry):
        return lagged_blocks(it * blocks_per_trip, blocks_per_trip, *carry)

    n_lag_trips = n_kv // blocks_per_trip
    lag_state = lax.fori_loop(0, n_lag_trips, lagged_trip, lag_state)
    n_left = n_kv - n_lag_trips * blocks_per_trip
    if n_left:
        lag_state = lagged_blocks(n_kv - n_left, n_left, *lag_state)
    write_out()
    worst_excess = jnp.max(jnp.maximum(lag_state[1][0], lag_state[1][1]))

    def scores(j, slot):
        col_max = []
        for hh in range(2):
            s_t = jnp.dot(k_block(j, hh), qt_ref[0, hh], preferred_element_type=F32)
            s_sc[slot, hh] = s_t
            col_max.append(jnp.max(s_t, axis=0, keepdims=True))
        return tuple(col_max)

    def consume(j, slot, col_max, m_run):
        new = []
        for hh in range(2):
            m_new = jnp.maximum(m_run[hh], col_max[hh])
            alpha = jnp.exp2(m_run[hh] - m_new)
            p = jnp.exp2(s_sc[slot, hh] - m_new)
            pv = jnp.dot(vt_ref[0, hh, j], p.astype(BF16), preferred_element_type=F32)
            acc_sc[hh] = alpha * acc_sc[hh] + pv
            new.append(m_new)
        return tuple(new)

    def pair(j, max_even, state):
        max_odd = scores(j + 1, 1)
        state = consume(j, 0, max_even, state)
        max_even = scores(j + 2, 0)
        state = consume(j + 1, 1, max_odd, state)
        return max_even, state

    def pair_trip(it, carry):
        for u in range(pairs_per_trip):
            carry = pair(2 * (it * pairs_per_trip + u), *carry)
        return carry

    @pl.when(worst_excess > LAG_LIMIT)
    def _():
        acc_sc[...] = jnp.zeros_like(acc_sc)
        state = tuple(jnp.full((1, tq), -jnp.inf, F32) for _ in range(2))
        max_even = scores(0, 0)
        n_full_pairs = n_kv // 2 - 1
        n_trips = n_full_pairs // pairs_per_trip
        max_even, state = lax.fori_loop(0, n_trips, pair_trip, (max_even, state))
        for p in range(n_trips * pairs_per_trip, n_full_pairs):
            max_even, state = pair(2 * p, max_even, state)
        max_odd = scores(n_kv - 1, 1)
        state = consume(n_kv - 2, 0, max_even, state)
        consume(n_kv - 1, 1, max_odd, state)
        write_out()


def _attention(q_t, k, v_t, tq, name):
    batch, heads, _, seq = q_t.shape
    n_kv, tk = v_t.shape[2], v_t.shape[4]
    assert n_kv % 2 == 0, "the kv loop is unrolled in pairs of blocks"
    pairs = heads // 2
    return pl.pallas_call(
        functools.partial(_attn_kernel, tk=tk, n_kv=n_kv, pairs_per_trip=ATTN_PAIRS_PER_TRIP,
                          blocks_per_trip=ATTN_BLOCKS_PER_TRIP),
        grid=(batch, pairs, seq // tq),
        in_specs=[pl.BlockSpec((1, 2, HEAD_PAD, tq), lambda b, hp, qi: (b, hp, 0, qi)),
                  pl.BlockSpec((1, seq, 2 * HEAD_PAD), lambda b, hp, qi: (b, 0, hp)),
                  pl.BlockSpec((1, 2, n_kv, V_ROWS, tk), lambda b, hp, qi: (b, hp, 0, 0, 0))],
        out_specs=pl.BlockSpec((1, tq, 2 * V_HEAD_DIM), lambda b, hp, qi: (b, qi, hp)),
        out_shape=jax.ShapeDtypeStruct((batch, seq, heads * V_HEAD_DIM), BF16),
        scratch_shapes=[pltpu.VMEM((2, 2, tk, tq), F32), pltpu.VMEM((2, V_ROWS, tq), F32)],
        compiler_params=_params(("parallel", "parallel", "arbitrary")),
        name=name,
    )(q_t, k, v_t)


def _mla_in_weight(w_in):
    c_q = w_in[:, :Q_LORA_RANK]
    c_kv = w_in[:, Q_LORA_RANK:Q_LORA_RANK + KV_LORA_RANK]
    k_rope = w_in[:, Q_LORA_RANK + KV_LORA_RANK:Q_LORA_RANK + KV_LORA_RANK + QK_ROPE_DIM]
    u_mem = w_in[:, Q_LORA_RANK + KV_LORA_RANK + QK_ROPE_DIM:]
    return jnp.concatenate([c_q, c_kv, u_mem, _slot_pad(None, k_rope)], axis=1).astype(BF16)


def kernel(x, mem, positions, ffn1_norm, ffn1_w_gate_up, ffn1_w_down, mix_norm, mem_norm, w_mem_kv,
           mem_q_norm, mem_k_norm, w_out, ffn2_norm, ffn2_w_gate_up, ffn2_w_down, lru_w_in, lru_conv_w,
           lru_conv_b, lru_gate_w, lru_gate_b, lru_lambda, mla_w_in, mla_q_a_norm, mla_w_uq,
           mla_kv_a_norm, mla_w_ukv, mla_q_norm, mla_k_norm):
    batch, seq, d = x.shape
    depth = ffn1_norm.shape[0]
    t = batch * seq
    xf = x.reshape(t, d)
    tables = _rope_tables(positions)
    tk = _token_tile(seq)

    for layer in range(depth):
        j = layer // 2
        if layer % 2 == 0:
            w_in, splits = lru_w_in[j].astype(BF16), (LRU_WIDTH, LRU_WIDTH, MEM_WIDTH)
        else:
            w_in, splits = _mla_in_weight(mla_w_in[j]), (Q_LORA_RANK, KV_LORA_RANK, MEM_WIDTH, HEAD_PAD)
        xf, *u = _ffn(xf, ffn1_norm[layer], ffn1_w_gate_up[layer].astype(BF16), ffn1_w_down[layer].astype(BF16),
                      f"ffn1_in_{layer}", mix_norm[layer], w_in, splits)
        kt, v = _mem_kv(mem, mem_norm[layer], w_mem_kv[layer].astype(BF16), mem_k_norm[layer],
                        f"mem_kv_{layer}")
        if layer % 2 == 0:
            gate, xr, u_mem = u
            gw = jnp.stack([
                jnp.concatenate([_dense_block_diag(lru_gate_w[j, dr, 0]),
                                 _dense_block_diag(lru_gate_w[j, dr, 1])], axis=1)
                for dr in range(2)]).astype(BF16)
            h_dirs = _lru_scan(xr.reshape(batch, seq, LRU_WIDTH), lru_conv_w[j], lru_conv_b[j], gw,
                               lru_gate_b[j], lru_lambda[j], f"lru_scan_{layer}")
            toks = (h_dirs.reshape(2, t, LRU_WIDTH), gate)
        else:
            cq, ckv, u_mem, kr = u
            q_t, k, v_t = _mla_prep(cq, ckv, kr, tables, mla_q_a_norm[j], mla_kv_a_norm[j],
                                    mla_q_norm[j], mla_k_norm[j], mla_w_uq[j], mla_w_ukv[j],
                                    batch, seq, tk, f"mla_prep_{layer}")
            attn = _attention(q_t, k, v_t, tk, f"mla_attn_{layer}")
            toks = (attn.reshape(t, TOK_WIDTH),)
        xf = _mix_out(xf, toks, u_mem, mem_q_norm[layer], kt, v, w_out[layer].astype(BF16), seq,
                      f"mix_out_{layer}")
        xf = _ffn(xf, ffn2_norm[layer], ffn2_w_gate_up[layer].astype(BF16),
                  ffn2_w_down[layer].astype(BF16), f"ffn2_{layer}")
    return xf.reshape(batch, seq, d)
```
